```python
import jax, jax.numpy as jnp
from jax import lax
import numpy as np

D_MODEL = 1024
BATCH = 8
SEQ = 2048
DEPTH = 2
DEC_BATCH = 128
DEC_SEQ = 8
PAST_LEN = 16384
PAGE_SIZE = 128

N_BRANCH = 4
W_BR = D_MODEL // 2
POOL_WINDOWS = (2, 4, 8, 16)
N_POOL_GROUPS = 4
POOL_GW = W_BR // N_POOL_GROUPS
POOL_PREV = max(POOL_WINDOWS) - 1
SCONV_K = 3
CCONV_K = 31
CHUNK = 128
N_SGU_GROUPS = 4
SGU_GW = W_BR // N_SGU_GROUPS
N_MEM = 256
N_XHEADS = 4
XHEAD_DIM = D_MODEL // N_XHEADS
D_FF = 4 * D_MODEL
EPS = 1e-6
SPLIT_POINTS = (W_BR, 2 * W_BR, 3 * W_BR, 4 * W_BR, 6 * W_BR, 8 * W_BR)
PROJ_W = 8 * W_BR + N_BRANCH * D_MODEL

kernel_name = 'gated_parallel_pool_conv_sgu_conformer_decoder_step'


def rmsnorm(x, g):
    xf = x.astype(jnp.float32)
    y = xf * lax.rsqrt(jnp.mean(xf * xf, axis=-1, keepdims=True) + EPS)
    return (y * g.astype(jnp.float32)).astype(x.dtype)


def layernorm(x, g, b):
    xf = x.astype(jnp.float32)
    xc = xf - jnp.mean(xf, axis=-1, keepdims=True)
    var = jnp.mean(xc * xc, axis=-1, keepdims=True)
    y = xc * lax.rsqrt(var + EPS) * g.astype(jnp.float32) + b.astype(jnp.float32)
    return y.astype(x.dtype)


def pool_mix(ext, pos0, pool_w, pool_scale):
    L = ext.shape[1] - POOL_PREV
    ef = ext.astype(jnp.float32)
    c = jnp.cumsum(ef, axis=1)
    c = jnp.concatenate([jnp.zeros_like(c[:, :1]), c], axis=1)
    pos = pos0 + jnp.arange(L)
    outs = []
    for g, w in enumerate(POOL_WINDOWS):
        sl = slice(g * POOL_GW, (g + 1) * POOL_GW)
        win_sum = c[:, POOL_PREV + 1:, sl] - c[:, POOL_PREV + 1 - w:POOL_PREV + 1 - w + L, sl]
        cnt = jnp.minimum(pos + 1, w).astype(jnp.float32)[None, :, None]
        p = win_sum / cnt - ef[:, POOL_PREV:, sl]
        outs.append(p.astype(pool_w.dtype) @ pool_w[g])
    return jnp.concatenate(outs, axis=-1) * pool_scale


def causal_dwconv(ext, w):
    C = ext.shape[-1]
    return lax.conv_general_dilated(ext, w[:, None, :].astype(ext.dtype), window_strides=(1,), padding='VALID',
                                    dimension_numbers=('NWC', 'WIO', 'NWC'), feature_group_count=C)


def chunk_spatial_mix(v, w_s, b_s):
    Bn, L, _ = v.shape
    Lp = -(-L // CHUNK) * CHUNK
    vp = jnp.pad(v, ((0, 0), (0, Lp - L), (0, 0))).reshape(Bn, Lp // CHUNK, CHUNK, N_SGU_GROUPS, SGU_GW)
    causal = jnp.tril(jnp.ones((CHUNK, CHUNK), dtype=bool))
    ws = jnp.where(causal[None], w_s, jnp.zeros_like(w_s))
    s = jnp.einsum('gts,bnsgc->bntgc', ws, vp) + jnp.transpose(b_s)[None, None, :, :, None]
    return s.reshape(Bn, Lp, W_BR)[:, :L]


def mixer_block(h, pool_prev, sconv_prev, cconv_prev, pos0, w_in, pool_w, pool_scale, sconv_w,
                sgu_ln_g, sgu_ln_b, sgu_w, sgu_b, cconv_w, cconv_b, cconv_ln_g, cconv_ln_b,
                w_branch, b_gate, w_mix_out):
    Bn, L, _ = h.shape
    z = h @ w_in
    a_u, b_h, b_b, b_c, c_uv, d_in, g_in = jnp.split(z, SPLIT_POINTS, axis=-1)
    pool_ext = jnp.concatenate([pool_prev, a_u], axis=1)
    a_out = pool_mix(pool_ext, pos0, pool_w, pool_scale)
    new_pool = pool_ext[:, -POOL_PREV:]
    sc_ext = jnp.concatenate([sconv_prev, b_c * b_h], axis=1)
    b_out = b_b * causal_dwconv(sc_ext, sconv_w)
    new_sconv = sc_ext[:, -(SCONV_K - 1):]
    c_u, c_v = jnp.split(jax.nn.gelu(c_uv), 2, axis=-1)
    c_vn = layernorm(c_v, sgu_ln_g, sgu_ln_b)
    c_out = c_u * chunk_spatial_mix(c_vn, sgu_w, sgu_b)
    d_a, d_b = jnp.split(d_in, 2, axis=-1)
    cc_ext = jnp.concatenate([cconv_prev, d_a * jax.nn.sigmoid(d_b)], axis=1)
    d_c = causal_dwconv(cc_ext, cconv_w) + cconv_b
    d_out = jax.nn.silu(layernorm(d_c, cconv_ln_g, cconv_ln_b))
    new_cconv = cc_ext[:, -(CCONV_K - 1):]
    branches = jnp.stack([a_out, b_out, c_out, d_out], axis=2)
    proj = jnp.einsum('blkw,kwd->blkd', branches, w_branch)
    gates = jax.nn.sigmoid(g_in.reshape(Bn, L, N_BRANCH, D_MODEL) + b_gate)
    merged = jnp.sum(gates * proj, axis=2)
    return merged @ w_mix_out, new_pool, new_sconv, new_cconv, c_vn


def mem_kv(mem, g, w_k, w_v):
    Bn = mem.shape[0]
    m = rmsnorm(mem, g)
    k = (m @ w_k).reshape(Bn, N_MEM, N_XHEADS, XHEAD_DIM)
    v = (m @ w_v).reshape(Bn, N_MEM, N_XHEADS, XHEAD_DIM)
    return k, v


def cross_attend(h, k, v, w_q, w_o):
    Bn, L, _ = h.shape
    q = (h @ w_q).reshape(Bn, L, N_XHEADS, XHEAD_DIM)
    s = jnp.einsum('blhd,bmhd->bhlm', q, k).astype(jnp.float32) * (XHEAD_DIM ** -0.5)
    p = jax.nn.softmax(s, axis=-1).astype(v.dtype)
    o = jnp.einsum('bhlm,bmhd->blhd', p, v).reshape(Bn, L, D_MODEL)
    return o @ w_o


def sq_relu_mlp(h, w1, w2):
    return jnp.square(jax.nn.relu(h @ w1)) @ w2


def run_trunk(x, pool_prev, sconv_prev, cconv_prev, mem_k, mem_v, pos0, P):
    pools, sconvs, cconvs, vs = [], [], [], []
    for l in range(DEPTH):
        h = rmsnorm(x, P['norm_mix_g'][l])
        m, npool, nsc, ncc, vn = mixer_block(
            h, pool_prev[l], sconv_prev[l], cconv_prev[l], pos0,
            P['w_in'][l], P['pool_w'][l], P['pool_scale'][l], P['sconv_w'][l],
            P['sgu_ln_g'][l], P['sgu_ln_b'][l], P['sgu_w'][l], P['sgu_b'][l],
            P['cconv_w'][l], P['cconv_b'][l], P['cconv_ln_g'][l], P['cconv_ln_b'][l],
            P['w_branch'][l], P['b_gate'][l], P['w_mix_out'][l])
        x = x + m
        x = x + cross_attend(rmsnorm(x, P['norm_x_g'][l]), mem_k[l], mem_v[l], P['w_xq'][l], P['w_xo'][l])
        x = x + sq_relu_mlp(rmsnorm(x, P['norm_ffn_g'][l]), P['w_ff1'][l], P['w_ff2'][l])
        pools.append(npool)
        sconvs.append(nsc)
        cconvs.append(ncc)
        vs.append(vn)
    return rmsnorm(x, P['norm_f_g']), pools, sconvs, cconvs, vs


def setup_inputs(seed: int = 0) -> dict:
    key = jax.random.key(seed)
    ks = iter(jax.random.split(key, 48))
    def nrm(shape, scale):
        return scale * jax.random.normal(next(ks), shape, jnp.float32)
    def gain(shape):
        return 1.0 + 0.05 * jax.random.normal(next(ks), shape, jnp.float32)
    dD = D_MODEL ** -0.5
    return {
        'x_prompt': nrm((BATCH, SEQ, D_MODEL), 1.0),
        'x_sample': nrm((DEC_BATCH, DEC_SEQ, D_MODEL), 1.0),
        'state_pool': nrm((DEPTH, DEC_BATCH, POOL_PREV, W_BR), 1.0),
        'state_sconv': nrm((DEPTH, DEC_BATCH, SCONV_K - 1, W_BR), 1.0),
        'state_cconv': nrm((DEPTH, DEC_BATCH, CCONV_K - 1, W_BR), 0.5),
        'cache_mem_k': nrm((DEPTH, DEC_BATCH, N_MEM, N_XHEADS, XHEAD_DIM), 1.0),
        'cache_mem_v': nrm((DEPTH, DEC_BATCH, N_MEM, N_XHEADS, XHEAD_DIM), 1.0),
        'mem_prompt': nrm((BATCH, N_MEM, D_MODEL), 1.0),
        'norm_mix_g': gain((DEPTH, D_MODEL)),
        'w_in': nrm((DEPTH, D_MODEL, PROJ_W), dD),
        'pool_w': nrm((DEPTH, N_POOL_GROUPS, POOL_GW, POOL_GW), POOL_GW ** -0.5),
        'pool_scale': 1.0 + 0.1 * jax.random.normal(next(ks), (DEPTH, W_BR), jnp.float32),
        'sconv_w': nrm((DEPTH, SCONV_K, W_BR), SCONV_K ** -0.5),
        'sgu_ln_g': gain((DEPTH, W_BR)),
        'sgu_ln_b': nrm((DEPTH, W_BR), 0.02),
        'sgu_w': nrm((DEPTH, N_SGU_GROUPS, CHUNK, CHUNK), CHUNK ** -0.5),
        'sgu_b': 1.0 + 0.02 * jax.random.normal(next(ks), (DEPTH, N_SGU_GROUPS, CHUNK), jnp.float32),
        'cconv_w': nrm((DEPTH, CCONV_K, W_BR), CCONV_K ** -0.5),
        'cconv_b': nrm((DEPTH, W_BR), 0.02),
        'cconv_ln_g': gain((DEPTH, W_BR)),
        'cconv_ln_b': nrm((DEPTH, W_BR), 0.02),
        'w_branch': nrm((DEPTH, N_BRANCH, W_BR, D_MODEL), W_BR ** -0.5),
        'b_gate': nrm((DEPTH, N_BRANCH, D_MODEL), 0.02),
        'w_mix_out': nrm((DEPTH, D_MODEL, D_MODEL), dD),
        'norm_x_g': gain((DEPTH, D_MODEL)),
        'norm_mem_g': gain((DEPTH, D_MODEL)),
        'w_xq': nrm((DEPTH, D_MODEL, D_MODEL), dD),
        'w_xk': nrm((DEPTH, D_MODEL, D_MODEL), dD),
        'w_xv': nrm((DEPTH, D_MODEL, D_MODEL), dD),
        'w_xo': nrm((DEPTH, D_MODEL, D_MODEL), dD),
        'norm_ffn_g': gain((DEPTH, D_MODEL)),
        'w_ff1': nrm((DEPTH, D_MODEL, D_FF), dD),
        'w_ff2': nrm((DEPTH, D_FF, D_MODEL), D_FF ** -0.5),
        'norm_f_g': gain((D_MODEL,)),
    }


def reference(x_prompt, x_sample, state_pool, state_sconv, state_cconv, cache_mem_k, cache_mem_v, mem_prompt,
              norm_mix_g, w_in, pool_w, pool_scale, sconv_w, sgu_ln_g, sgu_ln_b, sgu_w, sgu_b,
              cconv_w, cconv_b, cconv_ln_g, cconv_ln_b, w_branch, b_gate, w_mix_out,
              norm_x_g, norm_mem_g, w_xq, w_xk, w_xv, w_xo, norm_ffn_g, w_ff1, w_ff2, norm_f_g):
    P = {'norm_mix_g': norm_mix_g, 'w_in': w_in, 'pool_w': pool_w, 'pool_scale': pool_scale,
         'sconv_w': sconv_w, 'sgu_ln_g': sgu_ln_g, 'sgu_ln_b': sgu_ln_b, 'sgu_w': sgu_w, 'sgu_b': sgu_b,
         'cconv_w': cconv_w, 'cconv_b': cconv_b, 'cconv_ln_g': cconv_ln_g, 'cconv_ln_b': cconv_ln_b,
         'w_branch': w_branch, 'b_gate': b_gate, 'w_mix_out': w_mix_out, 'norm_x_g': norm_x_g,
         'w_xq': w_xq, 'w_xo': w_xo, 'norm_ffn_g': norm_ffn_g, 'w_ff1': w_ff1, 'w_ff2': w_ff2,
         'norm_f_g': norm_f_g}
    dt = x_prompt.dtype
    kv = [mem_kv(mem_prompt, norm_mem_g[l], w_xk[l], w_xv[l]) for l in range(DEPTH)]
    mem_k_prompt = jnp.stack([k for k, _ in kv])
    mem_v_prompt = jnp.stack([v for _, v in kv])
    zp = jnp.zeros((DEPTH, BATCH, POOL_PREV, W_BR), dt)
    zs = jnp.zeros((DEPTH, BATCH, SCONV_K - 1, W_BR), dt)
    zc = jnp.zeros((DEPTH, BATCH, CCONV_K - 1, W_BR), dt)
    y_prompt, pool_p, sconv_p, cconv_p, _ = run_trunk(x_prompt, zp, zs, zc, mem_k_prompt, mem_v_prompt, 0, P)
    y_sample, pool_s, sconv_s, cconv_s, v_s = run_trunk(x_sample, state_pool, state_sconv, state_cconv,
                                                        cache_mem_k, cache_mem_v, PAST_LEN, P)
    new_pool_prompt = jnp.stack(pool_p)
    new_sconv_prompt = jnp.stack(sconv_p)
    new_cconv_prompt = jnp.stack(cconv_p)
    new_pool_sample = jnp.stack(pool_s)
    new_sconv_sample = jnp.stack(sconv_s)
    new_cconv_sample = jnp.stack(cconv_s)
    sgu_v_sample = jnp.stack(v_s)
    return (y_prompt, y_sample, new_pool_prompt, new_sconv_prompt, new_cconv_prompt, mem_k_prompt, mem_v_prompt,
            new_pool_sample, new_sconv_sample, new_cconv_sample, sgu_v_sample)
```

```python
import functools

import jax
import jax.numpy as jnp
from jax import lax
from jax.experimental import pallas as pl
from jax.experimental.pallas import tpu as pltpu

D_MODEL = 1024
DEPTH = 2
PAST_LEN = 16384
W_BR = D_MODEL // 2
POOL_WINDOWS = (2, 4, 8, 16)
POOL_GW = W_BR // len(POOL_WINDOWS)
POOL_PREV = max(POOL_WINDOWS) - 1
SCONV_K = 3
CCONV_K = 31
CHUNK = 128
N_SGU_GROUPS = 4
SGU_GW = W_BR // N_SGU_GROUPS
N_MEM = 256
N_XHEADS = 4
XHEAD_DIM = D_MODEL // N_XHEADS
D_FF = 4 * D_MODEL
N_BRANCH = 4
EPS = 1e-6
GATE_COL0 = 8 * W_BR

SUBLANES = 8
POOL_OFF = 16
SCONV_OFF = 8
CCONV_OFF = 32

VMEM_LIMIT_BYTES = 56 * 1024 * 1024

BF16 = jnp.bfloat16
F32 = jnp.float32


def _dot(a, b):
    return jnp.dot(a, b, preferred_element_type=F32)


def _rmsnorm(x, g):
    return x * lax.rsqrt(jnp.mean(x * x, axis=-1, keepdims=True) + EPS) * g


def _layernorm(x, g, b):
    xc = x - jnp.mean(x, axis=-1, keepdims=True)
    var = jnp.mean(xc * xc, axis=-1, keepdims=True)
    return xc * lax.rsqrt(var + EPS) * g + b


def _gelu_tanh(x):
    return x * (0.5 * (1.0 + jnp.tanh(0.7978845608028654 * (x + 0.044715 * (x * x * x)))))


def _resident(shape):
    nd = len(shape)
    return pl.BlockSpec(shape, lambda *_: (0,) * nd, pipeline_mode=pl.Buffered(1))


def _mixer_kernel(x_ref, pool_st_ref, sconv_st_ref, cconv_st_ref,
                  g_mix_ref, w_in_ref, pool_w_ref, pool_scale_ref, sconv_w_ref,
                  sgu_g_ref, sgu_b_ref, sgu_w_ref, sgu_bcol_ref,
                  cconv_w_ref, cconv_b_ref, cln_g_ref, cln_b_ref,
                  w_br_ref, b_gate_ref, w_mix_ref,
                  y_ref, pool_out_ref, sconv_out_ref, cconv_out_ref, *rest,
                  nb, lc, nj, pos0, emit_v):
    if emit_v:
        v_out_ref, pool_ext, sconv_ext, cconv_ext = rest
    else:
        pool_ext, sconv_ext, cconv_ext = rest
    j = pl.program_id(1)
    tm = nb * lc

    x = x_ref[...].reshape(tm, D_MODEL)
    h = _rmsnorm(x, g_mix_ref[...]).astype(BF16)

    @pl.when(j == 0)
    def _load_history():
        pool_ext[:, POOL_OFF - POOL_PREV:POOL_OFF, :] = pool_st_ref[...]
        sconv_ext[:, SCONV_OFF - (SCONV_K - 1):SCONV_OFF, :] = sconv_st_ref[...]
        cconv_ext[:, CCONV_OFF - (CCONV_K - 1):CCONV_OFF, :] = cconv_st_ref[...]

    def gate(k):
        c0 = GATE_COL0 + k * D_MODEL
        return jax.nn.sigmoid(_dot(h, w_in_ref[:, c0:c0 + D_MODEL]) + b_gate_ref[k:k + 1, :])

    def project(branch, k):
        return _dot(branch.astype(BF16), w_br_ref[k])

    a_u = _dot(h, w_in_ref[:, 0:W_BR])
    pool_ext[:, POOL_OFF:POOL_OFF + lc, :] = a_u.reshape(nb, lc, W_BR)
    pos = pos0 + j * lc + lax.broadcasted_iota(jnp.int32, (1, lc, POOL_GW), 1)
    a_parts = []
    for g, w in enumerate(POOL_WINDOWS):
        sl = slice(g * POOL_GW, (g + 1) * POOL_GW)
        cur = pool_ext[:, POOL_OFF:POOL_OFF + lc, sl]
        win = cur
        for i in range(1, w):
            win = win + pool_ext[:, POOL_OFF - i:POOL_OFF - i + lc, sl]
        cnt = jnp.minimum(pos + 1, w).astype(F32)
        p = win / cnt - cur
        a_parts.append(_dot(p.reshape(tm, POOL_GW).astype(BF16), pool_w_ref[g]))
    a_out = jnp.concatenate(a_parts, axis=-1) * pool_scale_ref[...]
    new_pool = pool_ext[:, lc + POOL_OFF - POOL_PREV:lc + POOL_OFF, :]
    pool_out_ref[...] = new_pool
    if nj > 1:
        pool_ext[:, POOL_OFF - POOL_PREV:POOL_OFF, :] = new_pool
    merged = gate(0) * project(a_out, 0)

    z_b = _dot(h, w_in_ref[:, W_BR:4 * W_BR])
    b_h, b_b, b_c = z_b[:, :W_BR], z_b[:, W_BR:2 * W_BR], z_b[:, 2 * W_BR:]
    sconv_ext[:, SCONV_OFF:SCONV_OFF + lc, :] = (b_c * b_h).reshape(nb, lc, W_BR)
    s0 = SCONV_OFF - (SCONV_K - 1)
    conv = sconv_w_ref[0:1, :][None] * sconv_ext[:, s0:s0 + lc, :]
    for k in range(1, SCONV_K):
        conv = conv + sconv_w_ref[k:k + 1, :][None] * sconv_ext[:, s0 + k:s0 + k + lc, :]
    b_out = b_b * conv.reshape(tm, W_BR)
    new_sconv = sconv_ext[:, lc + s0:lc + SCONV_OFF, :]
    sconv_out_ref[...] = new_sconv
    if nj > 1:
        sconv_ext[:, s0:SCONV_OFF, :] = new_sconv
    merged = merged + gate(1) * project(b_out, 1)

    c_uv = _gelu_tanh(_dot(h, w_in_ref[:, 4 * W_BR:6 * W_BR]))
    c_u, c_v = c_uv[:, :W_BR], c_uv[:, W_BR:]
    c_vn = _layernorm(c_v, sgu_g_ref[...], sgu_b_ref[...])
    if emit_v:
        v_out_ref[...] = c_vn.reshape(nb, lc, W_BR)
    row = lax.broadcasted_iota(jnp.int32, (CHUNK, CHUNK), 0)
    col = lax.broadcasted_iota(jnp.int32, (CHUNK, CHUNK), 1)
    keep = row >= col
    seg = min(lc, CHUNK)
    if seg < CHUNK:
        keep = keep & ((row // seg) == (col // seg))
    c_vn_bf = c_vn.astype(BF16)
    s_cols = []
    for g in range(N_SGU_GROUPS):
        sl = slice(g * SGU_GW, (g + 1) * SGU_GW)
        w_g = jnp.where(keep, sgu_w_ref[g], 0.0).astype(BF16)
        bias = sgu_bcol_ref[:, g:g + 1]
        s_rows = [_dot(w_g, c_vn_bf[c * CHUNK:(c + 1) * CHUNK, sl]) + bias
                  for c in range(tm // CHUNK)]
        s_cols.append(jnp.concatenate(s_rows, axis=0))
    c_out = c_u * jnp.concatenate(s_cols, axis=-1)
    merged = merged + gate(2) * project(c_out, 2)

    d_in = _dot(h, w_in_ref[:, 6 * W_BR:8 * W_BR])
    glu = d_in[:, :W_BR] * jax.nn.sigmoid(d_in[:, W_BR:])
    cconv_ext[:, CCONV_OFF:CCONV_OFF + lc, :] = glu.reshape(nb, lc, W_BR)
    c0 = CCONV_OFF - (CCONV_K - 1)
    d_c = cconv_b_ref[...][None] + cconv_w_ref[0:1, :][None] * cconv_ext[:, c0:c0 + lc, :]
    for k in range(1, CCONV_K):
        d_c = d_c + cconv_w_ref[k:k + 1, :][None] * cconv_ext[:, c0 + k:c0 + k + lc, :]
    d_ln = _layernorm(d_c.reshape(tm, W_BR), cln_g_ref[...], cln_b_ref[...])
    d_out = d_ln * jax.nn.sigmoid(d_ln)
    new_cconv = cconv_ext[:, lc + c0:lc + CCONV_OFF, :]
    cconv_out_ref[...] = new_cconv
    if nj > 1:
        cconv_ext[:, c0:CCONV_OFF, :] = new_cconv
    merged = merged + gate(3) * project(d_out, 3)

    y = x + _dot(merged.astype(BF16), w_mix_ref[...])
    y_ref[...] = y.reshape(nb, lc, D_MODEL)


def _mixer(x, pool_st, sconv_st, cconv_st, lw, *, nb, lc, pos0, emit_v):
    n_seq, seq_len, _ = x.shape
    assert n_seq % nb == 0 and seq_len % lc == 0 and (nb * lc) % CHUNK == 0
    assert lc % SUBLANES == 0 and (lc % CHUNK == 0 or CHUNK % lc == 0)
    assert lc == seq_len or lc >= CCONV_K - 1
    nj = seq_len // lc
    grid = (n_seq // nb, nj)

    def tile(rows):
        return pl.BlockSpec((nb, rows, W_BR), lambda b, j: (b, 0, 0))

    seq_tile = pl.BlockSpec((nb, lc, D_MODEL), lambda b, j: (b, j, 0))
    params = (lw['g_mix'], lw['w_in'], lw['pool_w'], lw['pool_scale'], lw['sconv_w'],
              lw['sgu_ln_g'], lw['sgu_ln_b'], lw['sgu_w'], lw['sgu_bcol'],
              lw['cconv_w'], lw['cconv_b'], lw['cconv_ln_g'], lw['cconv_ln_b'],
              lw['w_branch'], lw['b_gate'], lw['w_mix_out'])
    in_specs = [seq_tile, tile(POOL_PREV), tile(SCONV_K - 1), tile(CCONV_K - 1)]
    in_specs += [_resident(p.shape) for p in params]
    out_shape = [jax.ShapeDtypeStruct(x.shape, F32),
                 jax.ShapeDtypeStruct(pool_st.shape, F32),
                 jax.ShapeDtypeStruct(sconv_st.shape, F32),
                 jax.ShapeDtypeStruct(cconv_st.shape, F32)]
    out_specs = [seq_tile, tile(POOL_PREV), tile(SCONV_K - 1), tile(CCONV_K - 1)]
    if emit_v:
        out_shape.append(jax.ShapeDtypeStruct((n_seq, seq_len, W_BR), F32))
        out_specs.append(pl.BlockSpec((nb, lc, W_BR), lambda b, j: (b, j, 0)))
    return pl.pallas_call(
        functools.partial(_mixer_kernel, nb=nb, lc=lc, nj=nj, pos0=pos0, emit_v=emit_v),
        grid=grid, in_specs=in_specs, out_specs=out_specs, out_shape=out_shape,
        scratch_shapes=[pltpu.VMEM((nb, POOL_OFF + lc, W_BR), F32),
                        pltpu.VMEM((nb, SCONV_OFF + lc, W_BR), F32),
                        pltpu.VMEM((nb, CCONV_OFF + lc, W_BR), F32)],
        compiler_params=pltpu.CompilerParams(
            dimension_semantics=("arbitrary", "arbitrary"),
            vmem_limit_bytes=VMEM_LIMIT_BYTES),
        name="mixer",
    )(x, pool_st, sconv_st, cconv_st, *params)


def _attn_kernel(x_ref, k_ref, v_ref, g_ref, wq_ref, wo_ref, y_ref, o_buf, *, nb, lq):
    x = x_ref[...].reshape(nb * lq, D_MODEL)
    h = _rmsnorm(x, g_ref[...]).astype(BF16)
    q = _dot(h, wq_ref[...]).astype(BF16)
    for b in range(nb):
        k_b = k_ref[b].astype(BF16)
        v_b = v_ref[b].astype(BF16)
        for hd in range(N_XHEADS):
            sl = slice(hd * XHEAD_DIM, (hd + 1) * XHEAD_DIM)
            s = lax.dot_general(q[b * lq:(b + 1) * lq, sl], k_b[:, sl],
                                (((1,), (1,)), ((), ())), preferred_element_type=F32)
            s = s * (XHEAD_DIM ** -0.5)
            e = jnp.exp(s - jnp.max(s, axis=-1, keepdims=True))
            p = e / jnp.sum(e, axis=-1, keepdims=True)
            o_buf[b * lq:(b + 1) * lq, sl] = _dot(p.astype(BF16), v_b[:, sl])
    y = x + _dot(o_buf[...].astype(BF16), wo_ref[...])
    y_ref[...] = y.reshape(nb, lq, D_MODEL)


def _attn(x, mem_k, mem_v, g, w_q, w_o, *, nb, lq):
    n_seq, seq_len, _ = x.shape
    assert n_seq % nb == 0 and seq_len % lq == 0 and lq % SUBLANES == 0
    seq_tile = pl.BlockSpec((nb, lq, D_MODEL), lambda b, j: (b, j, 0))
    mem_tile = pl.BlockSpec((nb, N_MEM, D_MODEL), lambda b, j: (b, 0, 0))
    return pl.pallas_call(
        functools.partial(_attn_kernel, nb=nb, lq=lq),
        grid=(n_seq // nb, seq_len // lq),
        in_specs=[seq_tile, mem_tile, mem_tile,
                  _resident(g.shape), _resident(w_q.shape), _resident(w_o.shape)],
        out_specs=seq_tile,
        out_shape=jax.ShapeDtypeStruct(x.shape, F32),
        scratch_shapes=[pltpu.VMEM((nb * lq, D_MODEL), F32)],
        compiler_params=pltpu.CompilerParams(
            dimension_semantics=("arbitrary", "arbitrary"),
            vmem_limit_bytes=VMEM_LIMIT_BYTES),
        name="cross_attn",
    )(x, mem_k, mem_v, g, w_q, w_o)


def _ffn_kernel(x_ref, g_ref, w1_ref, w2_ref, gf_ref, y_ref, *, final_norm):
    x = x_ref[...]
    h = _rmsnorm(x, g_ref[...]).astype(BF16)
    y = x
    for c in range(D_FF // D_MODEL):
        cols = slice(c * D_MODEL, (c + 1) * D_MODEL)
        u = jnp.square(jnp.maximum(_dot(h, w1_ref[:, cols]), 0.0))
        y = y + _dot(u.astype(BF16), w2_ref[cols, :])
    if final_norm:
        y = _rmsnorm(y, gf_ref[...])
    y_ref[...] = y


def _ffn(x, g, w1, w2, g_final, *, tm, final_norm):
    t = x.shape[0]
    assert t % tm == 0
    row_tile = pl.BlockSpec((tm, D_MODEL), lambda i: (i, 0))
    return pl.pallas_call(
        functools.partial(_ffn_kernel, final_norm=final_norm),
        grid=(t // tm,),
        in_specs=[row_tile, _resident(g.shape), _resident(w1.shape), _resident(w2.shape),
                  _resident(g_final.shape)],
        out_specs=row_tile,
        out_shape=jax.ShapeDtypeStruct(x.shape, F32),
        compiler_params=pltpu.CompilerParams(
            dimension_semantics=("arbitrary",), vmem_limit_bytes=VMEM_LIMIT_BYTES),
        name="ffn",
    )(x, g, w1, w2, g_final)


def _mem_kv_kernel(m_ref, g_ref, wk_ref, wv_ref, k_ref, v_ref, kb_ref, vb_ref):
    m = _rmsnorm(m_ref[...], g_ref[...]).astype(BF16)
    k = _dot(m, wk_ref[...])
    v = _dot(m, wv_ref[...])
    k_ref[...] = k
    v_ref[...] = v
    kb_ref[...] = k.astype(BF16)
    vb_ref[...] = v.astype(BF16)


def _mem_kv(mem, g, w_k, w_v, *, tm):
    rows = mem.shape[0]
    assert rows % tm == 0
    row_tile = pl.BlockSpec((tm, D_MODEL), lambda l, i: (i, 0))
    per_layer_w = pl.BlockSpec((None, D_MODEL, D_MODEL), lambda l, i: (l, 0, 0))
    out_tile = pl.BlockSpec((None, tm, D_MODEL), lambda l, i: (l, i, 0))
    shape = (DEPTH, rows, D_MODEL)
    return pl.pallas_call(
        _mem_kv_kernel,
        grid=(DEPTH, rows // tm),
        in_specs=[row_tile, pl.BlockSpec((None, 1, D_MODEL), lambda l, i: (l, 0, 0)),
                  per_layer_w, per_layer_w],
        out_specs=[out_tile] * 4,
        out_shape=[jax.ShapeDtypeStruct(shape, F32), jax.ShapeDtypeStruct(shape, F32),
                   jax.ShapeDtypeStruct(shape, BF16), jax.ShapeDtypeStruct(shape, BF16)],
        compiler_params=pltpu.CompilerParams(
            dimension_semantics=("arbitrary", "arbitrary"),
            vmem_limit_bytes=VMEM_LIMIT_BYTES),
        name="mem_kv",
    )(mem, g, w_k, w_v)


def _run_trunk(x, pool_st, sconv_st, cconv_st, mem_k, mem_v, layers, g_final, *,
               pos0, mix_nb, mix_lc, attn_nb, attn_lq, ffn_tm, emit_v):
    n_seq, seq_len, _ = x.shape
    pools, sconvs, cconvs, vs = [], [], [], []
    for l, lw in enumerate(layers):
        outs = _mixer(x, pool_st[l], sconv_st[l], cconv_st[l], lw,
                      nb=mix_nb, lc=mix_lc, pos0=pos0, emit_v=emit_v)
        x = outs[0]
        pools.append(outs[1])
        sconvs.append(outs[2])
        cconvs.append(outs[3])
        if emit_v:
            vs.append(outs[4])
        x = _attn(x, mem_k[l], mem_v[l], lw['g_x'], lw['w_xq'], lw['w_xo'],
                  nb=attn_nb, lq=attn_lq)
        x = _ffn(x.reshape(n_seq * seq_len, D_MODEL), lw['g_ffn'], lw['w_ff1'], lw['w_ff2'],
                 g_final, tm=ffn_tm, final_norm=(l == len(layers) - 1))
        x = x.reshape(n_seq, seq_len, D_MODEL)
    return x, jnp.stack(pools), jnp.stack(sconvs), jnp.stack(cconvs), vs


def kernel(x_prompt, x_sample, state_pool, state_sconv, state_cconv, cache_mem_k, cache_mem_v, mem_prompt, norm_mix_g, w_in, pool_w, pool_scale, sconv_w, sgu_ln_g, sgu_ln_b, sgu_w, sgu_b, cconv_w, cconv_b, cconv_ln_g, cconv_ln_b, w_branch, b_gate, w_mix_out, norm_x_g, norm_mem_g, w_xq, w_xk, w_xv, w_xo, norm_ffn_g, w_ff1, w_ff2, norm_f_g):
    batch, seq, _ = x_prompt.shape
    dec_batch, dec_seq, _ = x_sample.shape

    def row(v):
        return v.reshape(1, -1)

    def layer_weights(l, sgu_w_l, sgu_bcol_l):
        return {
            'g_mix': row(norm_mix_g[l]), 'w_in': w_in[l].astype(BF16),
            'pool_w': pool_w[l].astype(BF16), 'pool_scale': row(pool_scale[l]),
            'sconv_w': sconv_w[l], 'sgu_ln_g': row(sgu_ln_g[l]), 'sgu_ln_b': row(sgu_ln_b[l]),
            'sgu_w': sgu_w_l, 'sgu_bcol': sgu_bcol_l,
            'cconv_w': cconv_w[l], 'cconv_b': row(cconv_b[l]),
            'cconv_ln_g': row(cconv_ln_g[l]), 'cconv_ln_b': row(cconv_ln_b[l]),
            'w_branch': w_branch[l].astype(BF16), 'b_gate': b_gate[l],
            'w_mix_out': w_mix_out[l].astype(BF16),
            'g_x': row(norm_x_g[l]), 'w_xq': w_xq[l].astype(BF16), 'w_xo': w_xo[l].astype(BF16),
            'g_ffn': row(norm_ffn_g[l]), 'w_ff1': w_ff1[l].astype(BF16),
            'w_ff2': w_ff2[l].astype(BF16),
        }

    layers_p = [layer_weights(l, sgu_w[l], sgu_b[l].T) for l in range(DEPTH)]
    reps = CHUNK // dec_seq
    layers_s = [dict(layers_p[l],
                     sgu_w=jnp.tile(sgu_w[l][:, :dec_seq, :dec_seq], (1, reps, reps)),
                     sgu_bcol=jnp.tile(sgu_b[l][:, :dec_seq], (1, reps)).T)
                for l in range(DEPTH)]
    g_final = row(norm_f_g)

    mk, mv, mk_bf, mv_bf = _mem_kv(mem_prompt.reshape(batch * N_MEM, D_MODEL),
                                   norm_mem_g.reshape(DEPTH, 1, D_MODEL),
                                   w_xk.astype(BF16), w_xv.astype(BF16), tm=512)
    mem_shape = (DEPTH, batch, N_MEM, N_XHEADS, XHEAD_DIM)
    mem_k_prompt = mk.reshape(mem_shape)
    mem_v_prompt = mv.reshape(mem_shape)
    zeros = lambda rows: jnp.zeros((DEPTH, batch, rows, W_BR), F32)
    y_prompt, pool_p, sconv_p, cconv_p, _ = _run_trunk(
        x_prompt, zeros(POOL_PREV), zeros(SCONV_K - 1), zeros(CCONV_K - 1),
        mk_bf.reshape(DEPTH, batch, N_MEM, D_MODEL), mv_bf.reshape(DEPTH, batch, N_MEM, D_MODEL),
        layers_p, g_final, pos0=0, mix_nb=1, mix_lc=256, attn_nb=1, attn_lq=512, ffn_tm=512,
        emit_v=False)

    y_sample, pool_s, sconv_s, cconv_s, v_s = _run_trunk(
        x_sample, state_pool, state_sconv, state_cconv,
        cache_mem_k.reshape(DEPTH, dec_batch, N_MEM, D_MODEL),
        cache_mem_v.reshape(DEPTH, dec_batch, N_MEM, D_MODEL),
        layers_s, g_final, pos0=PAST_LEN, mix_nb=256 // dec_seq, mix_lc=dec_seq,
        attn_nb=4, attn_lq=dec_seq, ffn_tm=512, emit_v=True)

    return (y_prompt, y_sample, pool_p, sconv_p, cconv_p, mem_k_prompt, mem_v_prompt,
            pool_s, sconv_s, cconv_s, jnp.stack(v_s))
```

```python
import functools

import jax
import jax.numpy as jnp
from jax import lax
from jax.experimental import pallas as pl
from jax.experimental.pallas import tpu as pltpu

D_MODEL = 1024
DEPTH = 2
PAST_LEN = 16384
W_BR = D_MODEL // 2
POOL_WINDOWS = (2, 4, 8, 16)
POOL_GW = W_BR // len(POOL_WINDOWS)
POOL_PREV = max(POOL_WINDOWS) - 1
SCONV_K = 3
CCONV_K = 31
CHUNK = 128
N_SGU_GROUPS = 4
SGU_GW = W_BR // N_SGU_GROUPS
N_MEM = 256
N_XHEADS = 4
XHEAD_DIM = D_MODEL // N_XHEADS
D_FF = 4 * D_MODEL
N_BRANCH = 4
EPS = 1e-6
GATE_COL0 = 8 * W_BR

SUBLANES = 8
LANES = 128
N_PLANES = W_BR // LANES
POOL_OFF = 16
SCONV_OFF = 8
CCONV_OFF = 32

VMEM_LIMIT_BYTES = 56 * 1024 * 1024

BF16 = jnp.bfloat16
F32 = jnp.float32


def _dot(a, b):
    return jnp.dot(a, b, preferred_element_type=F32)


def _rmsnorm(x, g):
    return x * lax.rsqrt(jnp.mean(x * x, axis=-1, keepdims=True) + EPS) * g


def _layernorm(x, g, b):
    xc = x - jnp.mean(x, axis=-1, keepdims=True)
    var = jnp.mean(xc * xc, axis=-1, keepdims=True)
    return xc * lax.rsqrt(var + EPS) * g + b


def _gelu_tanh(x):
    return x * (0.5 * (1.0 + jnp.tanh(0.7978845608028654 * (x + 0.044715 * (x * x * x)))))


def _resident(shape):
    nd = len(shape)
    return pl.BlockSpec(shape, lambda *_: (0,) * nd, pipeline_mode=pl.Buffered(1))


def _store_planes(ext, row0, value):
    rows = value.shape[1]
    for c in range(N_PLANES):
        ext[c, :, row0:row0 + rows, :] = value[:, :, c * LANES:(c + 1) * LANES]


def _load_planes(ext, row0, rows):
    return jnp.concatenate([ext[c, :, row0:row0 + rows, :] for c in range(N_PLANES)], axis=-1)


def _causal_dwconv(ext, w_ref, row0, taps, rows):
    planes = []
    for c in range(N_PLANES):
        cs = slice(c * LANES, (c + 1) * LANES)
        acc = w_ref[0:1, cs][None] * ext[c, :, row0:row0 + rows, :]
        for k in range(1, taps):
            acc = acc + w_ref[k:k + 1, cs][None] * ext[c, :, row0 + k:row0 + k + rows, :]
        planes.append(acc)
    return jnp.concatenate(planes, axis=-1)


def _mixer_kernel(x_ref, pool_st_ref, sconv_st_ref, cconv_st_ref,
                  g_mix_ref, w_in_ref, pool_w_ref, pool_scale_ref, sconv_w_ref,
                  sgu_g_ref, sgu_b_ref, sgu_w_ref, sgu_bcol_ref,
                  cconv_w_ref, cconv_b_ref, cln_g_ref, cln_b_ref,
                  w_br_ref, b_gate_ref, w_mix_ref,
                  y_ref, pool_out_ref, sconv_out_ref, cconv_out_ref, *rest,
                  nb, lc, nj, pos0, emit_v):
    if emit_v:
        v_out_ref, pool_ext, sconv_ext, cconv_ext = rest
    else:
        pool_ext, sconv_ext, cconv_ext = rest
    j = pl.program_id(1)
    tm = nb * lc
    pool_h0 = POOL_OFF - POOL_PREV
    sconv_h0 = SCONV_OFF - (SCONV_K - 1)
    cconv_h0 = CCONV_OFF - (CCONV_K - 1)

    x = x_ref[...].reshape(tm, D_MODEL)
    h = _rmsnorm(x, g_mix_ref[...]).astype(BF16)

    @pl.when(j == 0)
    def _load_history():
        _store_planes(pool_ext, pool_h0, pool_st_ref[...])
        _store_planes(sconv_ext, sconv_h0, sconv_st_ref[...])
        _store_planes(cconv_ext, cconv_h0, cconv_st_ref[...])

    def carry(ext, h0, prev, out_ref):
        new = _load_planes(ext, lc + h0, prev)
        out_ref[...] = new
        if nj > 1:
            _store_planes(ext, h0, new)

    def gate(k):
        c0 = GATE_COL0 + k * D_MODEL
        return jax.nn.sigmoid(_dot(h, w_in_ref[:, c0:c0 + D_MODEL]) + b_gate_ref[k:k + 1, :])

    def project(branch, k):
        return _dot(branch.astype(BF16), w_br_ref[k])

    d_in = _dot(h, w_in_ref[:, 6 * W_BR:8 * W_BR])
    glu = d_in[:, :W_BR] * jax.nn.sigmoid(d_in[:, W_BR:])
    _store_planes(cconv_ext, CCONV_OFF, glu.reshape(nb, lc, W_BR))
    d_c = _causal_dwconv(cconv_ext, cconv_w_ref, cconv_h0, CCONV_K, lc).reshape(tm, W_BR)
    d_ln = _layernorm(d_c + cconv_b_ref[...], cln_g_ref[...], cln_b_ref[...])
    d_out = d_ln * jax.nn.sigmoid(d_ln)
    carry(cconv_ext, cconv_h0, CCONV_K - 1, cconv_out_ref)
    merged = gate(3) * project(d_out, 3)

    c_uv = _gelu_tanh(_dot(h, w_in_ref[:, 4 * W_BR:6 * W_BR]))
    c_u, c_v = c_uv[:, :W_BR], c_uv[:, W_BR:]
    c_vn = _layernorm(c_v, sgu_g_ref[...], sgu_b_ref[...])
    if emit_v:
        v_out_ref[...] = c_vn.reshape(nb, lc, W_BR)
    row = lax.broadcasted_iota(jnp.int32, (CHUNK, CHUNK), 0)
    col = lax.broadcasted_iota(jnp.int32, (CHUNK, CHUNK), 1)
    keep = row >= col
    seg = min(lc, CHUNK)
    if seg < CHUNK:
        keep = keep & ((row // seg) == (col // seg))
    c_vn_bf = c_vn.astype(BF16)
    s_cols = []
    for g in range(N_SGU_GROUPS):
        sl = slice(g * SGU_GW, (g + 1) * SGU_GW)
        w_g = jnp.where(keep, sgu_w_ref[g], 0.0).astype(BF16)
        bias = sgu_bcol_ref[:, g:g + 1]
        s_rows = [_dot(w_g, c_vn_bf[c * CHUNK:(c + 1) * CHUNK, sl]) + bias
                  for c in range(tm // CHUNK)]
        s_cols.append(jnp.concatenate(s_rows, axis=0))
    c_out = c_u * jnp.concatenate(s_cols, axis=-1)
    merged = merged + gate(2) * project(c_out, 2)

    z_b = _dot(h, w_in_ref[:, W_BR:4 * W_BR])
    b_h, b_b, b_c = z_b[:, :W_BR], z_b[:, W_BR:2 * W_BR], z_b[:, 2 * W_BR:]
    _store_planes(sconv_ext, SCONV_OFF, (b_c * b_h).reshape(nb, lc, W_BR))
    conv = _causal_dwconv(sconv_ext, sconv_w_ref, sconv_h0, SCONV_K, lc)
    b_out = b_b * conv.reshape(tm, W_BR)
    carry(sconv_ext, sconv_h0, SCONV_K - 1, sconv_out_ref)
    merged = merged + gate(1) * project(b_out, 1)

    a_u = _dot(h, w_in_ref[:, 0:W_BR])
    _store_planes(pool_ext, POOL_OFF, a_u.reshape(nb, lc, W_BR))
    pos = pos0 + j * lc + lax.broadcasted_iota(jnp.int32, (1, lc, POOL_GW), 1)
    a_parts = []
    for g, w in enumerate(POOL_WINDOWS):
        cur = pool_ext[g, :, POOL_OFF:POOL_OFF + lc, :]
        win = cur
        for i in range(1, w):
            win = win + pool_ext[g, :, POOL_OFF - i:POOL_OFF - i + lc, :]
        cnt = jnp.minimum(pos + 1, w).astype(F32)
        p = win / cnt - cur
        a_parts.append(_dot(p.reshape(tm, POOL_GW).astype(BF16), pool_w_ref[g]))
    a_out = jnp.concatenate(a_parts, axis=-1) * pool_scale_ref[...]
    carry(pool_ext, pool_h0, POOL_PREV, pool_out_ref)
    merged = merged + gate(0) * project(a_out, 0)

    y = x + _dot(merged.astype(BF16), w_mix_ref[...])
    y_ref[...] = y.reshape(nb, lc, D_MODEL)


def _mixer(x, pool_st, sconv_st, cconv_st, lw, *, nb, lc, pos0, emit_v):
    n_seq, seq_len, _ = x.shape
    assert n_seq % nb == 0 and seq_len % lc == 0 and (nb * lc) % CHUNK == 0
    assert lc % SUBLANES == 0 and (lc % CHUNK == 0 or CHUNK % lc == 0)
    assert lc == seq_len or lc >= CCONV_K - 1
    nj = seq_len // lc
    grid = (n_seq // nb, nj)

    def tile(rows):
        return pl.BlockSpec((nb, rows, W_BR), lambda b, j: (b, 0, 0))

    seq_tile = pl.BlockSpec((nb, lc, D_MODEL), lambda b, j: (b, j, 0))
    params = (lw['g_mix'], lw['w_in'], lw['pool_w'], lw['pool_scale'], lw['sconv_w'],
              lw['sgu_ln_g'], lw['sgu_ln_b'], lw['sgu_w'], lw['sgu_bcol'],
              lw['cconv_w'], lw['cconv_b'], lw['cconv_ln_g'], lw['cconv_ln_b'],
              lw['w_branch'], lw['b_gate'], lw['w_mix_out'])
    in_specs = [seq_tile, tile(POOL_PREV), tile(SCONV_K - 1), tile(CCONV_K - 1)]
    in_specs += [_resident(p.shape) for p in params]
    out_shape = [jax.ShapeDtypeStruct(x.shape, F32),
                 jax.ShapeDtypeStruct(pool_st.shape, F32),
                 jax.ShapeDtypeStruct(sconv_st.shape, F32),
                 jax.ShapeDtypeStruct(cconv_st.shape, F32)]
    out_specs = [seq_tile, tile(POOL_PREV), tile(SCONV_K - 1), tile(CCONV_K - 1)]
    if emit_v:
        out_shape.append(jax.ShapeDtypeStruct((n_seq, seq_len, W_BR), F32))
        out_specs.append(pl.BlockSpec((nb, lc, W_BR), lambda b, j: (b, j, 0)))
    return pl.pallas_call(
        functools.partial(_mixer_kernel, nb=nb, lc=lc, nj=nj, pos0=pos0, emit_v=emit_v),
        grid=grid, in_specs=in_specs, out_specs=out_specs, out_shape=out_shape,
        scratch_shapes=[pltpu.VMEM((N_PLANES, nb, POOL_OFF + lc, LANES), F32),
                        pltpu.VMEM((N_PLANES, nb, SCONV_OFF + lc, LANES), F32),
                        pltpu.VMEM((N_PLANES, nb, CCONV_OFF + lc, LANES), F32)],
        compiler_params=pltpu.CompilerParams(
            dimension_semantics=("arbitrary", "arbitrary"),
            vmem_limit_bytes=VMEM_LIMIT_BYTES),
        name="mixer",
    )(x, pool_st, sconv_st, cconv_st, *params)


def _attn_kernel(x_ref, klo_ref, khi_ref, vlo_ref, vhi_ref, g_ref, wq_ref, wo_ref, y_ref,
                 k_buf, v_buf, o_buf, *, nb, lq):
    j = pl.program_id(1)
    half = XHEAD_DIM // 2

    @pl.when(j == 0)
    def _gather_heads():
        for lo_ref, hi_ref, buf in ((klo_ref, khi_ref, k_buf), (vlo_ref, vhi_ref, v_buf)):
            lo = lo_ref.reshape(nb * N_MEM * N_XHEADS, half)
            hi = hi_ref.reshape(nb * N_MEM * N_XHEADS, half)
            for b in range(nb):
                for hd in range(N_XHEADS):
                    rows = pl.ds(b * N_MEM * N_XHEADS + hd, N_MEM, stride=N_XHEADS)
                    c0 = hd * XHEAD_DIM
                    buf[b, :, c0:c0 + half] = lo[rows, :].astype(BF16)
                    buf[b, :, c0 + half:c0 + XHEAD_DIM] = hi[rows, :].astype(BF16)

    x = x_ref[...].reshape(nb * lq, D_MODEL)
    h = _rmsnorm(x, g_ref[...]).astype(BF16)
    q = _dot(h, wq_ref[...]).astype(BF16)
    for b in range(nb):
        for hd in range(N_XHEADS):
            sl = slice(hd * XHEAD_DIM, (hd + 1) * XHEAD_DIM)
            s = lax.dot_general(q[b * lq:(b + 1) * lq, sl], k_buf[b, :, sl],
                                (((1,), (1,)), ((), ())), preferred_element_type=F32)
            s = s * (XHEAD_DIM ** -0.5)
            e = jnp.exp(s - jnp.max(s, axis=-1, keepdims=True))
            p = e / jnp.sum(e, axis=-1, keepdims=True)
            o_buf[b * lq:(b + 1) * lq, sl] = _dot(p.astype(BF16), v_buf[b, :, sl])
    y = x + _dot(o_buf[...].astype(BF16), wo_ref[...])
    y_ref[...] = y.reshape(nb, lq, D_MODEL)


def _attn(x, mem_k, mem_v, layer, g, w_q, w_o, *, nb, lq):
    n_seq, seq_len, _ = x.shape
    assert n_seq % nb == 0 and seq_len % lq == 0 and lq % SUBLANES == 0
    assert XHEAD_DIM == 2 * LANES
    seq_tile = pl.BlockSpec((nb, lq, D_MODEL), lambda b, j: (b, j, 0))

    def mem_half(part):
        return pl.BlockSpec((None, nb, N_MEM, N_XHEADS, LANES),
                            lambda b, j: (layer, b, 0, 0, part))

    return pl.pallas_call(
        functools.partial(_attn_kernel, nb=nb, lq=lq),
        grid=(n_seq // nb, seq_len // lq),
        in_specs=[seq_tile, mem_half(0), mem_half(1), mem_half(0), mem_half(1),
                  _resident(g.shape), _resident(w_q.shape), _resident(w_o.shape)],
        out_specs=seq_tile,
        out_shape=jax.ShapeDtypeStruct(x.shape, F32),
        scratch_shapes=[pltpu.VMEM((nb, N_MEM, D_MODEL), BF16),
                        pltpu.VMEM((nb, N_MEM, D_MODEL), BF16),
                        pltpu.VMEM((nb * lq, D_MODEL), F32)],
        compiler_params=pltpu.CompilerParams(
            dimension_semantics=("arbitrary", "arbitrary"),
            vmem_limit_bytes=VMEM_LIMIT_BYTES),
        name="cross_attn",
    )(x, mem_k, mem_k, mem_v, mem_v, g, w_q, w_o)


def _ffn_kernel(x_ref, g_ref, w1_ref, w2_ref, gf_ref, y_ref, *, final_norm):
    x = x_ref[...]
    h = _rmsnorm(x, g_ref[...]).astype(BF16)
    y = x
    for c in range(D_FF // D_MODEL):
        cols = slice(c * D_MODEL, (c + 1) * D_MODEL)
        u = jnp.square(jnp.maximum(_dot(h, w1_ref[:, cols]), 0.0))
        y = y + _dot(u.astype(BF16), w2_ref[cols, :])
    if final_norm:
        y = _rmsnorm(y, gf_ref[...])
    y_ref[...] = y


def _ffn(x, g, w1, w2, g_final, *, tm, final_norm):
    t = x.shape[0]
    assert t % tm == 0
    row_tile = pl.BlockSpec((tm, D_MODEL), lambda i: (i, 0))
    return pl.pallas_call(
        functools.partial(_ffn_kernel, final_norm=final_norm),
        grid=(t // tm,),
        in_specs=[row_tile, _resident(g.shape), _resident(w1.shape), _resident(w2.shape),
                  _resident(g_final.shape)],
        out_specs=row_tile,
        out_shape=jax.ShapeDtypeStruct(x.shape, F32),
        compiler_params=pltpu.CompilerParams(
            dimension_semantics=("arbitrary",), vmem_limit_bytes=VMEM_LIMIT_BYTES),
        name="ffn",
    )(x, g, w1, w2, g_final)


def _mem_kv_kernel(m_ref, g_ref, wk_ref, wv_ref, k_ref, v_ref):
    m = _rmsnorm(m_ref[...], g_ref[...]).astype(BF16)
    k_ref[...] = _dot(m, wk_ref[...])
    v_ref[...] = _dot(m, wv_ref[...])


def _mem_kv(mem, g, w_k, w_v, *, tm):
    rows = mem.shape[0]
    assert rows % tm == 0
    row_tile = pl.BlockSpec((tm, D_MODEL), lambda l, i: (i, 0))
    per_layer_w = pl.BlockSpec((None, D_MODEL, D_MODEL), lambda l, i: (l, 0, 0))
    out_tile = pl.BlockSpec((None, tm, D_MODEL), lambda l, i: (l, i, 0))
    shape = (DEPTH, rows, D_MODEL)
    return pl.pallas_call(
        _mem_kv_kernel,
        grid=(DEPTH, rows // tm),
        in_specs=[row_tile, pl.BlockSpec((None, 1, D_MODEL), lambda l, i: (l, 0, 0)),
                  per_layer_w, per_layer_w],
        out_specs=[out_tile] * 2,
        out_shape=[jax.ShapeDtypeStruct(shape, F32), jax.ShapeDtypeStruct(shape, F32)],
        compiler_params=pltpu.CompilerParams(
            dimension_semantics=("arbitrary", "arbitrary"),
            vmem_limit_bytes=VMEM_LIMIT_BYTES),
        name="mem_kv",
    )(mem, g, w_k, w_v)


def _run_trunk(x, pool_st, sconv_st, cconv_st, mem_k, mem_v, layers, g_final, *,
               pos0, mix_nb, mix_lc, attn_nb, attn_lq, ffn_tm, emit_v):
    n_seq, seq_len, _ = x.shape
    pools, sconvs, cconvs, vs = [], [], [], []
    for l, lw in enumerate(layers):
        outs = _mixer(x, pool_st[l], sconv_st[l], cconv_st[l], lw,
                      nb=mix_nb, lc=mix_lc, pos0=pos0, emit_v=emit_v)
        x = outs[0]
        pools.append(outs[1])
        sconvs.append(outs[2])
        cconvs.append(outs[3])
        if emit_v:
            vs.append(outs[4])
        x = _attn(x, mem_k, mem_v, l, lw['g_x'], lw['w_xq'], lw['w_xo'],
                  nb=attn_nb, lq=attn_lq)
        x = _ffn(x.reshape(n_seq * seq_len, D_MODEL), lw['g_ffn'], lw['w_ff1'], lw['w_ff2'],
                 g_final, tm=ffn_tm, final_norm=(l == len(layers) - 1))
        x = x.reshape(n_seq, seq_len, D_MODEL)
    return x, jnp.stack(pools), jnp.stack(sconvs), jnp.stack(cconvs), vs


def kernel(x_prompt, x_sample, state_pool, state_sconv, state_cconv, cache_mem_k, cache_mem_v, mem_prompt, norm_mix_g, w_in, pool_w, pool_scale, sconv_w, sgu_ln_g, sgu_ln_b, sgu_w, sgu_b, cconv_w, cconv_b, cconv_ln_g, cconv_ln_b, w_branch, b_gate, w_mix_out, norm_x_g, norm_mem_g, w_xq, w_xk, w_xv, w_xo, norm_ffn_g, w_ff1, w_ff2, norm_f_g):
    batch, seq, _ = x_prompt.shape
    dec_batch, dec_seq, _ = x_sample.shape

    def row(v):
        return v.reshape(1, -1)

    def layer_weights(l, sgu_w_l, sgu_bcol_l):
        return {
            'g_mix': row(norm_mix_g[l]), 'w_in': w_in[l].astype(BF16),
            'pool_w': pool_w[l].astype(BF16), 'pool_scale': row(pool_scale[l]),
            'sconv_w': sconv_w[l], 'sgu_ln_g': row(sgu_ln_g[l]), 'sgu_ln_b': row(sgu_ln_b[l]),
            'sgu_w': sgu_w_l, 'sgu_bcol': sgu_bcol_l,
            'cconv_w': cconv_w[l], 'cconv_b': row(cconv_b[l]),
            'cconv_ln_g': row(cconv_ln_g[l]), 'cconv_ln_b': row(cconv_ln_b[l]),
            'w_branch': w_branch[l].astype(BF16), 'b_gate': b_gate[l],
            'w_mix_out': w_mix_out[l].astype(BF16),
            'g_x': row(norm_x_g[l]), 'w_xq': w_xq[l].astype(BF16), 'w_xo': w_xo[l].astype(BF16),
            'g_ffn': row(norm_ffn_g[l]), 'w_ff1': w_ff1[l].astype(BF16),
            'w_ff2': w_ff2[l].astype(BF16),
        }

    layers_p = [layer_weights(l, sgu_w[l], sgu_b[l].T) for l in range(DEPTH)]
    reps = CHUNK // dec_seq
    layers_s = [dict(layers_p[l],
                     sgu_w=jnp.tile(sgu_w[l][:, :dec_seq, :dec_seq], (1, reps, reps)),
                     sgu_bcol=jnp.tile(sgu_b[l][:, :dec_seq], (1, reps)).T)
                for l in range(DEPTH)]
    g_final = row(norm_f_g)

    mk, mv = _mem_kv(mem_prompt.reshape(batch * N_MEM, D_MODEL),
                     norm_mem_g.reshape(DEPTH, 1, D_MODEL),
                     w_xk.astype(BF16), w_xv.astype(BF16), tm=512)
    mem_shape = (DEPTH, batch, N_MEM, N_XHEADS, XHEAD_DIM)
    mem_k_prompt = mk.reshape(mem_shape)
    mem_v_prompt = mv.reshape(mem_shape)
    zeros = lambda rows: jnp.zeros((DEPTH, batch, rows, W_BR), F32)
    y_prompt, pool_p, sconv_p, cconv_p, _ = _run_trunk(
        x_prompt, zeros(POOL_PREV), zeros(SCONV_K - 1), zeros(CCONV_K - 1),
        mem_k_prompt, mem_v_prompt, layers_p, g_final,
        pos0=0, mix_nb=1, mix_lc=256, attn_nb=1, attn_lq=512, ffn_tm=512, emit_v=False)

    y_sample, pool_s, sconv_s, cconv_s, v_s = _run_trunk(
        x_sample, state_pool, state_sconv, state_cconv, cache_mem_k, cache_mem_v,
        layers_s, g_final, pos0=PAST_LEN, mix_nb=256 // dec_seq, mix_lc=dec_seq,
        attn_nb=8, attn_lq=dec_seq, ffn_tm=512, emit_v=True)

    return (y_prompt, y_sample, pool_p, sconv_p, cconv_p, mem_k_prompt, mem_v_prompt,
            pool_s, sconv_s, cconv_s, jnp.stack(v_s))
```

```python
import functools

import jax
import jax.numpy as jnp
from jax import lax
from jax.experimental import pallas as pl
from jax.experimental.pallas import tpu as pltpu

D_MODEL = 1024
DEPTH = 2
PAST_LEN = 16384
W_BR = D_MODEL // 2
POOL_WINDOWS = (2, 4, 8, 16)
POOL_GW = W_BR // len(POOL_WINDOWS)
POOL_PREV = max(POOL_WINDOWS) - 1
SCONV_K = 3
CCONV_K = 31
CHUNK = 128
N_SGU_GROUPS = 4
SGU_GW = W_BR // N_SGU_GROUPS
N_MEM = 256
N_XHEADS = 4
XHEAD_DIM = D_MODEL // N_XHEADS
D_FF = 4 * D_MODEL
N_BRANCH = 4
EPS = 1e-6
GATE_COL0 = 8 * W_BR

SUBLANES = 8
LANES = 128
N_PLANES = W_BR // LANES
POOL_OFF = 16
SCONV_OFF = 8
CCONV_OFF = 32

VMEM_LIMIT_BYTES = 56 * 1024 * 1024

BF16 = jnp.bfloat16
F32 = jnp.float32


def _dot(a, b):
    return jnp.dot(a, b, preferred_element_type=F32)


def _pack_weight(w):
    *lead, k, n = w.shape
    pairs = w.astype(BF16).reshape(*lead, k // 2, 2, n)
    return lax.bitcast_convert_type(jnp.swapaxes(pairs, -1, -2), jnp.uint32)


def _wdot(a, w_packed):
    return _dot(a, pltpu.bitcast(w_packed, BF16))


def _rmsnorm(x, g):
    return x * lax.rsqrt(jnp.mean(x * x, axis=-1, keepdims=True) + EPS) * g


def _layernorm(x, g, b):
    xc = x - jnp.mean(x, axis=-1, keepdims=True)
    var = jnp.mean(xc * xc, axis=-1, keepdims=True)
    return xc * lax.rsqrt(var + EPS) * g + b


def _gelu_tanh(x):
    return x * (0.5 * (1.0 + jnp.tanh(0.7978845608028654 * (x + 0.044715 * (x * x * x)))))


def _resident(shape):
    nd = len(shape)
    return pl.BlockSpec(shape, lambda *_: (0,) * nd, pipeline_mode=pl.Buffered(1))


def _store_planes(ext, row0, value):
    rows = value.shape[1]
    for c in range(N_PLANES):
        ext[c, :, row0:row0 + rows, :] = value[:, :, c * LANES:(c + 1) * LANES]


def _load_planes(ext, row0, rows):
    return jnp.concatenate([ext[c, :, row0:row0 + rows, :] for c in range(N_PLANES)], axis=-1)


def _causal_dwconv(ext, w_ref, row0, taps, rows):
    planes = []
    for c in range(N_PLANES):
        cs = slice(c * LANES, (c + 1) * LANES)
        acc = w_ref[0:1, cs][None] * ext[c, :, row0:row0 + rows, :]
        for k in range(1, taps):
            acc = acc + w_ref[k:k + 1, cs][None] * ext[c, :, row0 + k:row0 + k + rows, :]
        planes.append(acc)
    return jnp.concatenate(planes, axis=-1)


def _mixer_kernel(x_ref, pool_st_ref, sconv_st_ref, cconv_st_ref,
                  g_mix_ref, w_in_ref, pool_w_ref, pool_scale_ref, sconv_w_ref,
                  sgu_g_ref, sgu_b_ref, sgu_w_ref, sgu_bcol_ref,
                  cconv_w_ref, cconv_b_ref, cln_g_ref, cln_b_ref,
                  w_br_ref, b_gate_ref, w_mix_ref,
                  y_ref, pool_out_ref, sconv_out_ref, cconv_out_ref, *rest,
                  nb, lc, nj, pos0, emit_v):
    if emit_v:
        v_out_ref, pool_ext, sconv_ext, cconv_ext = rest
    else:
        pool_ext, sconv_ext, cconv_ext = rest
    j = pl.program_id(1)
    tm = nb * lc
    pool_h0 = POOL_OFF - POOL_PREV
    sconv_h0 = SCONV_OFF - (SCONV_K - 1)
    cconv_h0 = CCONV_OFF - (CCONV_K - 1)

    x = x_ref[...].reshape(tm, D_MODEL)
    h = _rmsnorm(x, g_mix_ref[...]).astype(BF16)

    @pl.when(j == 0)
    def _load_history():
        _store_planes(pool_ext, pool_h0, pool_st_ref[...])
        _store_planes(sconv_ext, sconv_h0, sconv_st_ref[...])
        _store_planes(cconv_ext, cconv_h0, cconv_st_ref[...])

    def carry(ext, h0, prev, out_ref):
        new = _load_planes(ext, lc + h0, prev)
        out_ref[...] = new
        if nj > 1:
            _store_planes(ext, h0, new)

    d_in = _wdot(h, w_in_ref[:, 6 * W_BR:8 * W_BR])
    c_uv = _wdot(h, w_in_ref[:, 4 * W_BR:6 * W_BR])
    z_b = _wdot(h, w_in_ref[:, W_BR:4 * W_BR])
    a_u = _wdot(h, w_in_ref[:, 0:W_BR])
    gates = [jax.nn.sigmoid(_wdot(h, w_in_ref[:, GATE_COL0 + k * D_MODEL:GATE_COL0 + (k + 1) * D_MODEL])
                            + b_gate_ref[k:k + 1, :]) for k in range(N_BRANCH)]

    glu = d_in[:, :W_BR] * jax.nn.sigmoid(d_in[:, W_BR:])
    _store_planes(cconv_ext, CCONV_OFF, glu.reshape(nb, lc, W_BR))
    d_c = _causal_dwconv(cconv_ext, cconv_w_ref, cconv_h0, CCONV_K, lc).reshape(tm, W_BR)
    d_ln = _layernorm(d_c + cconv_b_ref[...], cln_g_ref[...], cln_b_ref[...])
    d_out = d_ln * jax.nn.sigmoid(d_ln)
    carry(cconv_ext, cconv_h0, CCONV_K - 1, cconv_out_ref)

    c_uv = _gelu_tanh(c_uv)
    c_u, c_v = c_uv[:, :W_BR], c_uv[:, W_BR:]
    c_vn = _layernorm(c_v, sgu_g_ref[...], sgu_b_ref[...])
    if emit_v:
        v_out_ref[...] = c_vn.reshape(nb, lc, W_BR)
    row = lax.broadcasted_iota(jnp.int32, (CHUNK, CHUNK), 0)
    col = lax.broadcasted_iota(jnp.int32, (CHUNK, CHUNK), 1)
    keep = row >= col
    seg = min(lc, CHUNK)
    if seg < CHUNK:
        keep = keep & ((row // seg) == (col // seg))
    c_vn_bf = c_vn.astype(BF16)
    s_cols = []
    for g in range(N_SGU_GROUPS):
        sl = slice(g * SGU_GW, (g + 1) * SGU_GW)
        w_g = jnp.where(keep, sgu_w_ref[g], 0.0).astype(BF16)
        bias = sgu_bcol_ref[:, g:g + 1]
        s_rows = [_dot(w_g, c_vn_bf[c * CHUNK:(c + 1) * CHUNK, sl]) + bias
                  for c in range(tm // CHUNK)]
        s_cols.append(jnp.concatenate(s_rows, axis=0))
    c_out = c_u * jnp.concatenate(s_cols, axis=-1)

    b_h, b_b, b_c = z_b[:, :W_BR], z_b[:, W_BR:2 * W_BR], z_b[:, 2 * W_BR:]
    _store_planes(sconv_ext, SCONV_OFF, (b_c * b_h).reshape(nb, lc, W_BR))
    conv = _causal_dwconv(sconv_ext, sconv_w_ref, sconv_h0, SCONV_K, lc)
    b_out = b_b * conv.reshape(tm, W_BR)
    carry(sconv_ext, sconv_h0, SCONV_K - 1, sconv_out_ref)

    _store_planes(pool_ext, POOL_OFF, a_u.reshape(nb, lc, W_BR))
    pos = pos0 + j * lc + lax.broadcasted_iota(jnp.int32, (1, lc, POOL_GW), 1)
    a_parts = []
    for g, w in enumerate(POOL_WINDOWS):
        cur = pool_ext[g, :, POOL_OFF:POOL_OFF + lc, :]
        win = cur
        for i in range(1, w):
            win = win + pool_ext[g, :, POOL_OFF - i:POOL_OFF - i + lc, :]
        cnt = jnp.minimum(pos + 1, w).astype(F32)
        p = win / cnt - cur
        a_parts.append(_dot(p.reshape(tm, POOL_GW).astype(BF16), pool_w_ref[g]))
    a_out = jnp.concatenate(a_parts, axis=-1) * pool_scale_ref[...]
    carry(pool_ext, pool_h0, POOL_PREV, pool_out_ref)

    merged = None
    for k, branch in enumerate((a_out, b_out, c_out, d_out)):
        term = gates[k] * _wdot(branch.astype(BF16), w_br_ref[k])
        merged = term if merged is None else merged + term

    y = x + _wdot(merged.astype(BF16), w_mix_ref[...])
    y_ref[...] = y.reshape(nb, lc, D_MODEL)


def _mixer(x, pool_st, sconv_st, cconv_st, lw, *, nb, lc, pos0, emit_v):
    n_seq, seq_len, _ = x.shape
    assert n_seq % nb == 0 and seq_len % lc == 0 and (nb * lc) % CHUNK == 0
    assert lc % SUBLANES == 0 and (lc % CHUNK == 0 or CHUNK % lc == 0)
    assert lc == seq_len or lc >= CCONV_K - 1
    nj = seq_len // lc
    grid = (n_seq // nb, nj)

    def tile(rows):
        return pl.BlockSpec((nb, rows, W_BR), lambda b, j: (b, 0, 0))

    seq_tile = pl.BlockSpec((nb, lc, D_MODEL), lambda b, j: (b, j, 0))
    params = (lw['g_mix'], lw['w_in'], lw['pool_w'], lw['pool_scale'], lw['sconv_w'],
              lw['sgu_ln_g'], lw['sgu_ln_b'], lw['sgu_w'], lw['sgu_bcol'],
              lw['cconv_w'], lw['cconv_b'], lw['cconv_ln_g'], lw['cconv_ln_b'],
              lw['w_branch'], lw['b_gate'], lw['w_mix_out'])
    in_specs = [seq_tile, tile(POOL_PREV), tile(SCONV_K - 1), tile(CCONV_K - 1)]
    in_specs += [_resident(p.shape) for p in params]
    out_shape = [jax.ShapeDtypeStruct(x.shape, F32),
                 jax.ShapeDtypeStruct(pool_st.shape, F32),
                 jax.ShapeDtypeStruct(sconv_st.shape, F32),
                 jax.ShapeDtypeStruct(cconv_st.shape, F32)]
    out_specs = [seq_tile, tile(POOL_PREV), tile(SCONV_K - 1), tile(CCONV_K - 1)]
    if emit_v:
        out_shape.append(jax.ShapeDtypeStruct((n_seq, seq_len, W_BR), F32))
        out_specs.append(pl.BlockSpec((nb, lc, W_BR), lambda b, j: (b, j, 0)))
    return pl.pallas_call(
        functools.partial(_mixer_kernel, nb=nb, lc=lc, nj=nj, pos0=pos0, emit_v=emit_v),
        grid=grid, in_specs=in_specs, out_specs=out_specs, out_shape=out_shape,
        scratch_shapes=[pltpu.VMEM((N_PLANES, nb, POOL_OFF + lc, LANES), F32),
                        pltpu.VMEM((N_PLANES, nb, SCONV_OFF + lc, LANES), F32),
                        pltpu.VMEM((N_PLANES, nb, CCONV_OFF + lc, LANES), F32)],
        compiler_params=pltpu.CompilerParams(
            dimension_semantics=("arbitrary", "arbitrary"),
            vmem_limit_bytes=VMEM_LIMIT_BYTES),
        name="mixer",
    )(x, pool_st, sconv_st, cconv_st, *params)


def _attn_kernel(x_ref, klo_ref, khi_ref, vlo_ref, vhi_ref, g_ref, wq_ref, wo_ref, y_ref,
                 k_buf, v_buf, o_buf, *, nb, lq):
    j = pl.program_id(1)
    half = XHEAD_DIM // 2

    @pl.when(j == 0)
    def _gather_heads():
        for lo_ref, hi_ref, buf in ((klo_ref, khi_ref, k_buf), (vlo_ref, vhi_ref, v_buf)):
            lo = lo_ref.reshape(nb * N_MEM * N_XHEADS, half)
            hi = hi_ref.reshape(nb * N_MEM * N_XHEADS, half)
            for b in range(nb):
                for hd in range(N_XHEADS):
                    rows = pl.ds(b * N_MEM * N_XHEADS + hd, N_MEM, stride=N_XHEADS)
                    c0 = hd * XHEAD_DIM
                    buf[b, :, c0:c0 + half] = lo[rows, :].astype(BF16)
                    buf[b, :, c0 + half:c0 + XHEAD_DIM] = hi[rows, :].astype(BF16)

    x = x_ref[...].reshape(nb * lq, D_MODEL)
    h = _rmsnorm(x, g_ref[...]).astype(BF16)
    q = _wdot(h, wq_ref[...])
    pairs = [(b, hd) for b in range(nb) for hd in range(N_XHEADS)]
    cols = lambda hd: slice(hd * XHEAD_DIM, (hd + 1) * XHEAD_DIM)
    s = jnp.concatenate(
        [lax.dot_general(q[b * lq:(b + 1) * lq, cols(hd)].astype(BF16), k_buf[b, :, cols(hd)],
                         (((1,), (1,)), ((), ())), preferred_element_type=F32)
         for b, hd in pairs], axis=0) * (XHEAD_DIM ** -0.5)
    e = jnp.exp(s - jnp.max(s, axis=-1, keepdims=True))
    p = e / jnp.sum(e, axis=-1, keepdims=True)
    for i, (b, hd) in enumerate(pairs):
        o_buf[b * lq:(b + 1) * lq, cols(hd)] = _dot(p[i * lq:(i + 1) * lq].astype(BF16),
                                                    v_buf[b, :, cols(hd)])
    y = x + _wdot(o_buf[...].astype(BF16), wo_ref[...])
    y_ref[...] = y.reshape(nb, lq, D_MODEL)


def _attn(x, mem_k, mem_v, layer, g, w_q, w_o, *, nb, lq):
    n_seq, seq_len, _ = x.shape
    assert n_seq % nb == 0 and seq_len % lq == 0 and lq % SUBLANES == 0
    assert XHEAD_DIM == 2 * LANES
    seq_tile = pl.BlockSpec((nb, lq, D_MODEL), lambda b, j: (b, j, 0))

    def mem_half(part):
        return pl.BlockSpec((None, nb, N_MEM, N_XHEADS, LANES),
                            lambda b, j: (layer, b, 0, 0, part))

    return pl.pallas_call(
        functools.partial(_attn_kernel, nb=nb, lq=lq),
        grid=(n_seq // nb, seq_len // lq),
        in_specs=[seq_tile, mem_half(0), mem_half(1), mem_half(0), mem_half(1),
                  _resident(g.shape), _resident(w_q.shape), _resident(w_o.shape)],
        out_specs=seq_tile,
        out_shape=jax.ShapeDtypeStruct(x.shape, F32),
        scratch_shapes=[pltpu.VMEM((nb, N_MEM, D_MODEL), BF16),
                        pltpu.VMEM((nb, N_MEM, D_MODEL), BF16),
                        pltpu.VMEM((nb * lq, D_MODEL), F32)],
        compiler_params=pltpu.CompilerParams(
            dimension_semantics=("arbitrary", "arbitrary"),
            vmem_limit_bytes=VMEM_LIMIT_BYTES),
        name="cross_attn",
    )(x, mem_k, mem_k, mem_v, mem_v, g, w_q, w_o)


def _ffn_kernel(x_ref, g_ref, w1_ref, w2_ref, gf_ref, y_ref, *, final_norm):
    x = x_ref[...]
    h = _rmsnorm(x, g_ref[...]).astype(BF16)
    y = x
    for c in range(D_FF // D_MODEL):
        u = jnp.square(jnp.maximum(_wdot(h, w1_ref[:, c * D_MODEL:(c + 1) * D_MODEL]), 0.0))
        y = y + _wdot(u.astype(BF16), w2_ref[c * D_MODEL // 2:(c + 1) * D_MODEL // 2, :])
    if final_norm:
        y = _rmsnorm(y, gf_ref[...])
    y_ref[...] = y


def _ffn(x, g, w1, w2, g_final, *, tm, final_norm):
    t = x.shape[0]
    assert t % tm == 0
    row_tile = pl.BlockSpec((tm, D_MODEL), lambda i: (i, 0))
    return pl.pallas_call(
        functools.partial(_ffn_kernel, final_norm=final_norm),
        grid=(t // tm,),
        in_specs=[row_tile, _resident(g.shape), _resident(w1.shape), _resident(w2.shape),
                  _resident(g_final.shape)],
        out_specs=row_tile,
        out_shape=jax.ShapeDtypeStruct(x.shape, F32),
        compiler_params=pltpu.CompilerParams(
            dimension_semantics=("arbitrary",), vmem_limit_bytes=VMEM_LIMIT_BYTES),
        name="ffn",
    )(x, g, w1, w2, g_final)


def _mem_kv_kernel(m_ref, g_ref, wk_ref, wv_ref, k_ref, v_ref):
    m = _rmsnorm(m_ref[...], g_ref[...]).astype(BF16)
    k_ref[...] = _wdot(m, wk_ref[...])
    v_ref[...] = _wdot(m, wv_ref[...])


def _mem_kv(mem, g, w_k, w_v, *, tm):
    rows = mem.shape[0]
    assert rows % tm == 0
    row_tile = pl.BlockSpec((tm, D_MODEL), lambda l, i: (i, 0))
    per_layer_w = pl.BlockSpec((None, D_MODEL // 2, D_MODEL), lambda l, i: (l, 0, 0))
    out_tile = pl.BlockSpec((None, tm, D_MODEL), lambda l, i: (l, i, 0))
    shape = (DEPTH, rows, D_MODEL)
    return pl.pallas_call(
        _mem_kv_kernel,
        grid=(DEPTH, rows // tm),
        in_specs=[row_tile, pl.BlockSpec((None, 1, D_MODEL), lambda l, i: (l, 0, 0)),
                  per_layer_w, per_layer_w],
        out_specs=[out_tile] * 2,
        out_shape=[jax.ShapeDtypeStruct(shape, F32), jax.ShapeDtypeStruct(shape, F32)],
        compiler_params=pltpu.CompilerParams(
            dimension_semantics=("arbitrary", "arbitrary"),
            vmem_limit_bytes=VMEM_LIMIT_BYTES),
        name="mem_kv",
    )(mem, g, w_k, w_v)


def _run_trunk(x, pool_st, sconv_st, cconv_st, mem_k, mem_v, layers, g_final, *,
               pos0, mix_nb, mix_lc, attn_nb, attn_lq, ffn_tm, emit_v):
    n_seq, seq_len, _ = x.shape
    pools, sconvs, cconvs, vs = [], [], [], []
    for l, lw in enumerate(layers):
        outs = _mixer(x, pool_st[l], sconv_st[l], cconv_st[l], lw,
                      nb=mix_nb, lc=mix_lc, pos0=pos0, emit_v=emit_v)
        x = outs[0]
        pools.append(outs[1])
        sconvs.append(outs[2])
        cconvs.append(outs[3])
        if emit_v:
            vs.append(outs[4])
        x = _attn(x, mem_k, mem_v, l, lw['g_x'], lw['w_xq'], lw['w_xo'],
                  nb=attn_nb, lq=attn_lq)
        x = _ffn(x.reshape(n_seq * seq_len, D_MODEL), lw['g_ffn'], lw['w_ff1'], lw['w_ff2'],
                 g_final, tm=ffn_tm, final_norm=(l == len(layers) - 1))
        x = x.reshape(n_seq, seq_len, D_MODEL)
    return x, jnp.stack(pools), jnp.stack(sconvs), jnp.stack(cconvs), vs


def kernel(x_prompt, x_sample, state_pool, state_sconv, state_cconv, cache_mem_k, cache_mem_v, mem_prompt, norm_mix_g, w_in, pool_w, pool_scale, sconv_w, sgu_ln_g, sgu_ln_b, sgu_w, sgu_b, cconv_w, cconv_b, cconv_ln_g, cconv_ln_b, w_branch, b_gate, w_mix_out, norm_x_g, norm_mem_g, w_xq, w_xk, w_xv, w_xo, norm_ffn_g, w_ff1, w_ff2, norm_f_g):
    batch, seq, _ = x_prompt.shape
    dec_batch, dec_seq, _ = x_sample.shape

    def row(v):
        return v.reshape(1, -1)

    def layer_weights(l, sgu_w_l, sgu_bcol_l):
        return {
            'g_mix': row(norm_mix_g[l]), 'w_in': _pack_weight(w_in[l]),
            'pool_w': pool_w[l].astype(BF16), 'pool_scale': row(pool_scale[l]),
            'sconv_w': sconv_w[l], 'sgu_ln_g': row(sgu_ln_g[l]), 'sgu_ln_b': row(sgu_ln_b[l]),
            'sgu_w': sgu_w_l, 'sgu_bcol': sgu_bcol_l,
            'cconv_w': cconv_w[l], 'cconv_b': row(cconv_b[l]),
            'cconv_ln_g': row(cconv_ln_g[l]), 'cconv_ln_b': row(cconv_ln_b[l]),
            'w_branch': _pack_weight(w_branch[l]), 'b_gate': b_gate[l],
            'w_mix_out': _pack_weight(w_mix_out[l]),
            'g_x': row(norm_x_g[l]), 'w_xq': _pack_weight(w_xq[l]), 'w_xo': _pack_weight(w_xo[l]),
            'g_ffn': row(norm_ffn_g[l]), 'w_ff1': _pack_weight(w_ff1[l]),
            'w_ff2': _pack_weight(w_ff2[l]),
        }

    layers_p = [layer_weights(l, sgu_w[l], sgu_b[l].T) for l in range(DEPTH)]
    reps = CHUNK // dec_seq
    layers_s = [dict(layers_p[l],
                     sgu_w=jnp.tile(sgu_w[l][:, :dec_seq, :dec_seq], (1, reps, reps)),
                     sgu_bcol=jnp.tile(sgu_b[l][:, :dec_seq], (1, reps)).T)
                for l in range(DEPTH)]
    g_final = row(norm_f_g)

    mk, mv = _mem_kv(mem_prompt.reshape(batch * N_MEM, D_MODEL),
                     norm_mem_g.reshape(DEPTH, 1, D_MODEL),
                     _pack_weight(w_xk), _pack_weight(w_xv), tm=512)
    mem_shape = (DEPTH, batch, N_MEM, N_XHEADS, XHEAD_DIM)
    mem_k_prompt = mk.reshape(mem_shape)
    mem_v_prompt = mv.reshape(mem_shape)
    zeros = lambda rows: jnp.zeros((DEPTH, batch, rows, W_BR), F32)
    y_prompt, pool_p, sconv_p, cconv_p, _ = _run_trunk(
        x_prompt, zeros(POOL_PREV), zeros(SCONV_K - 1), zeros(CCONV_K - 1),
        mem_k_prompt, mem_v_prompt, layers_p, g_final,
        pos0=0, mix_nb=1, mix_lc=256, attn_nb=1, attn_lq=512, ffn_tm=512, emit_v=False)

    y_sample, pool_s, sconv_s, cconv_s, v_s = _run_trunk(
        x_sample, state_pool, state_sconv, state_cconv, cache_mem_k, cache_mem_v,
        layers_s, g_final, pos0=PAST_LEN, mix_nb=256 // dec_seq, mix_lc=dec_seq,
        attn_nb=8, attn_lq=dec_seq, ffn_tm=512, emit_v=True)

    return (y_prompt, y_sample, pool_p, sconv_p, cconv_p, mem_k_prompt, mem_v_prompt,
            pool_s, sconv_s, cconv_s, jnp.stack(v_s))
```

```python
import functools

import jax
import jax.numpy as jnp
from jax import lax
from jax.experimental import pallas as pl
from jax.experimental.pallas import tpu as pltpu

D_MODEL = 1024
DEPTH = 2
PAST_LEN = 16384
W_BR = D_MODEL // 2
POOL_WINDOWS = (2, 4, 8, 16)
POOL_GW = W_BR // len(POOL_WINDOWS)
POOL_PREV = max(POOL_WINDOWS) - 1
SCONV_K = 3
CCONV_K = 31
CHUNK = 128
N_SGU_GROUPS = 4
SGU_GW = W_BR // N_SGU_GROUPS
N_MEM = 256
N_XHEADS = 4
XHEAD_DIM = D_MODEL // N_XHEADS
D_FF = 4 * D_MODEL
N_BRANCH = 4
EPS = 1e-6
GATE_COL0 = 8 * W_BR

SUBLANES = 8
LANES = 128
N_PLANES = W_BR // LANES
POOL_OFF = 16
SCONV_OFF = 8
CCONV_OFF = 32

VMEM_LIMIT_BYTES = 56 * 1024 * 1024

BF16 = jnp.bfloat16
F32 = jnp.float32


def _dot(a, b):
    return jnp.dot(a, b, preferred_element_type=F32)


def _rmsnorm(x, g):
    return x * lax.rsqrt(jnp.mean(x * x, axis=-1, keepdims=True) + EPS) * g


def _layernorm(x, g, b):
    xc = x - jnp.mean(x, axis=-1, keepdims=True)
    var = jnp.mean(xc * xc, axis=-1, keepdims=True)
    return xc * lax.rsqrt(var + EPS) * g + b


def _gelu_tanh(x):
    return x * (0.5 * (1.0 + jnp.tanh(0.7978845608028654 * (x + 0.044715 * (x * x * x)))))


def _resident(shape):
    nd = len(shape)
    return pl.BlockSpec(shape, lambda *_: (0,) * nd, pipeline_mode=pl.Buffered(1))


def _store_planes(ext, row0, value):
    rows = value.shape[1]
    for c in range(N_PLANES):
        ext[c, :, row0:row0 + rows, :] = value[:, :, c * LANES:(c + 1) * LANES]


def _load_planes(ext, row0, rows):
    return jnp.concatenate([ext[c, :, row0:row0 + rows, :] for c in range(N_PLANES)], axis=-1)


def _causal_dwconv(ext, w_ref, row0, taps, rows):
    planes = []
    for c in range(N_PLANES):
        cs = slice(c * LANES, (c + 1) * LANES)
        acc = w_ref[0:1, cs][None] * ext[c, :, row0:row0 + rows, :]
        for k in range(1, taps):
            acc = acc + w_ref[k:k + 1, cs][None] * ext[c, :, row0 + k:row0 + k + rows, :]
        planes.append(acc)
    return jnp.concatenate(planes, axis=-1)


def _mixer_kernel(x_ref, pool_st_ref, sconv_st_ref, cconv_st_ref,
                  g_mix_ref, w_in_ref, pool_w_ref, pool_scale_ref, sconv_w_ref,
                  sgu_g_ref, sgu_b_ref, sgu_w_ref, sgu_bcol_ref,
                  cconv_w_ref, cconv_b_ref, cln_g_ref, cln_b_ref,
                  w_br_ref, b_gate_ref, w_mix_ref, *rest,
                  nb, lc, nj, pos0, emit_v, n_aliased):
    y_ref, pool_out_ref, sconv_out_ref, cconv_out_ref, *rest = rest[n_aliased:]
    if emit_v:
        v_out_ref, pool_ext, sconv_ext, cconv_ext = rest
    else:
        pool_ext, sconv_ext, cconv_ext = rest
    j = pl.program_id(1)
    tm = nb * lc
    pool_h0 = POOL_OFF - POOL_PREV
    sconv_h0 = SCONV_OFF - (SCONV_K - 1)
    cconv_h0 = CCONV_OFF - (CCONV_K - 1)

    x = x_ref[...].reshape(tm, D_MODEL)
    h = _rmsnorm(x, g_mix_ref[...]).astype(BF16)

    @pl.when(j == 0)
    def _load_history():
        _store_planes(pool_ext, pool_h0, pool_st_ref[...])
        _store_planes(sconv_ext, sconv_h0, sconv_st_ref[...])
        _store_planes(cconv_ext, cconv_h0, cconv_st_ref[...])

    def carry(ext, h0, prev, out_ref):
        new = _load_planes(ext, lc + h0, prev)
        out_ref[...] = new
        if nj > 1:
            _store_planes(ext, h0, new)

    d_in = _dot(h, w_in_ref[:, 6 * W_BR:8 * W_BR])
    c_uv = _dot(h, w_in_ref[:, 4 * W_BR:6 * W_BR])
    z_b = _dot(h, w_in_ref[:, W_BR:4 * W_BR])
    a_u = _dot(h, w_in_ref[:, 0:W_BR])
    gates = [jax.nn.sigmoid(_dot(h, w_in_ref[:, GATE_COL0 + k * D_MODEL:GATE_COL0 + (k + 1) * D_MODEL])
                            + b_gate_ref[k:k + 1, :]) for k in range(N_BRANCH)]

    glu = d_in[:, :W_BR] * jax.nn.sigmoid(d_in[:, W_BR:])
    _store_planes(cconv_ext, CCONV_OFF, glu.reshape(nb, lc, W_BR))
    d_c = _causal_dwconv(cconv_ext, cconv_w_ref, cconv_h0, CCONV_K, lc).reshape(tm, W_BR)
    d_ln = _layernorm(d_c + cconv_b_ref[...], cln_g_ref[...], cln_b_ref[...])
    d_out = d_ln * jax.nn.sigmoid(d_ln)
    carry(cconv_ext, cconv_h0, CCONV_K - 1, cconv_out_ref)

    c_uv = _gelu_tanh(c_uv)
    c_u, c_v = c_uv[:, :W_BR], c_uv[:, W_BR:]
    c_vn = _layernorm(c_v, sgu_g_ref[...], sgu_b_ref[...])
    if emit_v:
        v_out_ref[...] = c_vn.reshape(nb, lc, W_BR)
    row = lax.broadcasted_iota(jnp.int32, (CHUNK, CHUNK), 0)
    col = lax.broadcasted_iota(jnp.int32, (CHUNK, CHUNK), 1)
    keep = row >= col
    seg = min(lc, CHUNK)
    if seg < CHUNK:
        keep = keep & ((row // seg) == (col // seg))
    c_vn_bf = c_vn.astype(BF16)
    s_cols = []
    for g in range(N_SGU_GROUPS):
        sl = slice(g * SGU_GW, (g + 1) * SGU_GW)
        w_g = jnp.where(keep, sgu_w_ref[g], 0.0).astype(BF16)
        bias = sgu_bcol_ref[:, g:g + 1]
        s_rows = [_dot(w_g, c_vn_bf[c * CHUNK:(c + 1) * CHUNK, sl]) + bias
                  for c in range(tm // CHUNK)]
        s_cols.append(jnp.concatenate(s_rows, axis=0))
    c_out = c_u * jnp.concatenate(s_cols, axis=-1)

    b_h, b_b, b_c = z_b[:, :W_BR], z_b[:, W_BR:2 * W_BR], z_b[:, 2 * W_BR:]
    _store_planes(sconv_ext, SCONV_OFF, (b_c * b_h).reshape(nb, lc, W_BR))
    conv = _causal_dwconv(sconv_ext, sconv_w_ref, sconv_h0, SCONV_K, lc)
    b_out = b_b * conv.reshape(tm, W_BR)
    carry(sconv_ext, sconv_h0, SCONV_K - 1, sconv_out_ref)

    _store_planes(pool_ext, POOL_OFF, a_u.reshape(nb, lc, W_BR))
    pos = pos0 + j * lc + lax.broadcasted_iota(jnp.int32, (1, lc, POOL_GW), 1)
    a_parts = []
    for g, w in enumerate(POOL_WINDOWS):
        cur = pool_ext[g, :, POOL_OFF:POOL_OFF + lc, :]
        win = cur
        for i in range(1, w):
            win = win + pool_ext[g, :, POOL_OFF - i:POOL_OFF - i + lc, :]
        cnt = jnp.minimum(pos + 1, w).astype(F32)
        p = win / cnt - cur
        a_parts.append(_dot(p.reshape(tm, POOL_GW).astype(BF16), pool_w_ref[g]))
    a_out = jnp.concatenate(a_parts, axis=-1) * pool_scale_ref[...]
    carry(pool_ext, pool_h0, POOL_PREV, pool_out_ref)

    merged = None
    for k, branch in enumerate((a_out, b_out, c_out, d_out)):
        term = gates[k] * _dot(branch.astype(BF16), w_br_ref[k])
        merged = term if merged is None else merged + term

    y = x + _dot(merged.astype(BF16), w_mix_ref[...])
    y_ref[...] = y.reshape(nb, lc, D_MODEL)


def _mixer(x, pool_st, sconv_st, cconv_st, layer, stacked, lw, *, nb, lc, pos0, emit_v):
    n_seq, seq_len, _ = x.shape
    assert n_seq % nb == 0 and seq_len % lc == 0 and (nb * lc) % CHUNK == 0
    assert lc % SUBLANES == 0 and (lc % CHUNK == 0 or CHUNK % lc == 0)
    assert lc == seq_len or lc >= CCONV_K - 1
    nj = seq_len // lc
    grid = (n_seq // nb, nj)

    def state_tile(rows):
        return pl.BlockSpec((None, nb, rows, W_BR), lambda b, j: (layer, b, 0, 0))

    seq_tile = pl.BlockSpec((nb, lc, D_MODEL), lambda b, j: (b, j, 0))
    params = (lw['g_mix'], lw['w_in'], lw['pool_w'], lw['pool_scale'], lw['sconv_w'],
              lw['sgu_ln_g'], lw['sgu_ln_b'], lw['sgu_w'], lw['sgu_bcol'],
              lw['cconv_w'], lw['cconv_b'], lw['cconv_ln_g'], lw['cconv_ln_b'],
              lw['w_branch'], lw['b_gate'], lw['w_mix_out'])
    in_specs = [seq_tile, state_tile(POOL_PREV), state_tile(SCONV_K - 1), state_tile(CCONV_K - 1)]
    in_specs += [_resident(p.shape) for p in params]
    in_specs += [pl.BlockSpec(memory_space=pl.ANY)] * len(stacked)
    out_shape = [jax.ShapeDtypeStruct(x.shape, F32)]
    out_shape += [jax.ShapeDtypeStruct(s.shape, s.dtype) for s in stacked]
    out_specs = [seq_tile, state_tile(POOL_PREV), state_tile(SCONV_K - 1), state_tile(CCONV_K - 1)]
    if emit_v:
        out_specs.append(pl.BlockSpec((None, nb, lc, W_BR), lambda b, j: (layer, b, j, 0)))
    assert len(stacked) == len(out_specs) - 1
    first_stacked_in = 4 + len(params)
    outs = pl.pallas_call(
        functools.partial(_mixer_kernel, nb=nb, lc=lc, nj=nj, pos0=pos0, emit_v=emit_v,
                          n_aliased=len(stacked)),
        grid=grid, in_specs=in_specs, out_specs=out_specs, out_shape=out_shape,
        input_output_aliases={first_stacked_in + i: 1 + i for i in range(len(stacked))},
        scratch_shapes=[pltpu.VMEM((N_PLANES, nb, POOL_OFF + lc, LANES), F32),
                        pltpu.VMEM((N_PLANES, nb, SCONV_OFF + lc, LANES), F32),
                        pltpu.VMEM((N_PLANES, nb, CCONV_OFF + lc, LANES), F32)],
        compiler_params=pltpu.CompilerParams(
            dimension_semantics=("arbitrary", "arbitrary"),
            vmem_limit_bytes=VMEM_LIMIT_BYTES),
        name="mixer",
    )(x, pool_st, sconv_st, cconv_st, *params, *stacked)
    return outs[0], tuple(outs[1:])


def _attn_kernel(x_ref, klo_ref, khi_ref, vlo_ref, vhi_ref, g_ref, wq_ref, wo_ref, y_ref,
                 k_buf, v_buf, o_buf, *, nb, lq):
    j = pl.program_id(1)
    half = XHEAD_DIM // 2

    @pl.when(j == 0)
    def _gather_heads():
        for lo_ref, hi_ref, buf in ((klo_ref, khi_ref, k_buf), (vlo_ref, vhi_ref, v_buf)):
            lo = lo_ref.reshape(nb * N_MEM * N_XHEADS, half)
            hi = hi_ref.reshape(nb * N_MEM * N_XHEADS, half)
            for b in range(nb):
                for hd in range(N_XHEADS):
                    rows = pl.ds(b * N_MEM * N_XHEADS + hd, N_MEM, stride=N_XHEADS)
                    c0 = hd * XHEAD_DIM
                    buf[b, :, c0:c0 + half] = lo[rows, :].astype(BF16)
                    buf[b, :, c0 + half:c0 + XHEAD_DIM] = hi[rows, :].astype(BF16)

    x = x_ref[...].reshape(nb * lq, D_MODEL)
    h = _rmsnorm(x, g_ref[...]).astype(BF16)
    q = _dot(h, wq_ref[...])
    pairs = [(b, hd) for b in range(nb) for hd in range(N_XHEADS)]
    cols = lambda hd: slice(hd * XHEAD_DIM, (hd + 1) * XHEAD_DIM)
    s = jnp.concatenate(
        [lax.dot_general(q[b * lq:(b + 1) * lq, cols(hd)].astype(BF16), k_buf[b, :, cols(hd)],
                         (((1,), (1,)), ((), ())), preferred_element_type=F32)
         for b, hd in pairs], axis=0) * (XHEAD_DIM ** -0.5)
    e = jnp.exp(s - jnp.max(s, axis=-1, keepdims=True))
    p = e / jnp.sum(e, axis=-1, keepdims=True)
    for i, (b, hd) in enumerate(pairs):
        o_buf[b * lq:(b + 1) * lq, cols(hd)] = _dot(p[i * lq:(i + 1) * lq].astype(BF16),
                                                    v_buf[b, :, cols(hd)])
    y = x + _dot(o_buf[...].astype(BF16), wo_ref[...])
    y_ref[...] = y.reshape(nb, lq, D_MODEL)


def _attn(x, mem_k, mem_v, layer, g, w_q, w_o, *, nb, lq):
    n_seq, seq_len, _ = x.shape
    assert n_seq % nb == 0 and seq_len % lq == 0 and lq % SUBLANES == 0
    assert XHEAD_DIM == 2 * LANES
    seq_tile = pl.BlockSpec((nb, lq, D_MODEL), lambda b, j: (b, j, 0))

    def mem_half(part):
        return pl.BlockSpec((None, nb, N_MEM, N_XHEADS, LANES),
                            lambda b, j: (layer, b, 0, 0, part))

    return pl.pallas_call(
        functools.partial(_attn_kernel, nb=nb, lq=lq),
        grid=(n_seq // nb, seq_len // lq),
        in_specs=[seq_tile, mem_half(0), mem_half(1), mem_half(0), mem_half(1),
                  _resident(g.shape), _resident(w_q.shape), _resident(w_o.shape)],
        out_specs=seq_tile,
        out_shape=jax.ShapeDtypeStruct(x.shape, F32),
        scratch_shapes=[pltpu.VMEM((nb, N_MEM, D_MODEL), BF16),
                        pltpu.VMEM((nb, N_MEM, D_MODEL), BF16),
                        pltpu.VMEM((nb * lq, D_MODEL), F32)],
        compiler_params=pltpu.CompilerParams(
            dimension_semantics=("arbitrary", "arbitrary"),
            vmem_limit_bytes=VMEM_LIMIT_BYTES),
        name="cross_attn",
    )(x, mem_k, mem_k, mem_v, mem_v, g, w_q, w_o)


def _ffn_kernel(x_ref, g_ref, w1_ref, w2_ref, gf_ref, y_ref, *, final_norm):
    x = x_ref[...]
    h = _rmsnorm(x, g_ref[...]).astype(BF16)
    y = x
    for c in range(D_FF // D_MODEL):
        u = jnp.square(jnp.maximum(_dot(h, w1_ref[:, c * D_MODEL:(c + 1) * D_MODEL]), 0.0))
        y = y + _dot(u.astype(BF16), w2_ref[c * D_MODEL:(c + 1) * D_MODEL, :])
    if final_norm:
        y = _rmsnorm(y, gf_ref[...])
    y_ref[...] = y


def _ffn(x, g, w1, w2, g_final, *, tm, final_norm):
    t = x.shape[0]
    assert t % tm == 0
    row_tile = pl.BlockSpec((tm, D_MODEL), lambda i: (i, 0))
    return pl.pallas_call(
        functools.partial(_ffn_kernel, final_norm=final_norm),
        grid=(t // tm,),
        in_specs=[row_tile, _resident(g.shape), _resident(w1.shape), _resident(w2.shape),
                  _resident(g_final.shape)],
        out_specs=row_tile,
        out_shape=jax.ShapeDtypeStruct(x.shape, F32),
        compiler_params=pltpu.CompilerParams(
            dimension_semantics=("arbitrary",), vmem_limit_bytes=VMEM_LIMIT_BYTES),
        name="ffn",
    )(x, g, w1, w2, g_final)


def _mem_kv_kernel(m_ref, g_ref, wk_ref, wv_ref, k_ref, v_ref, *, nb):
    m = _rmsnorm(m_ref[...], g_ref[...]).astype(BF16)
    for w_ref, o_ref in ((wk_ref, k_ref), (wv_ref, v_ref)):
        y = _dot(m, w_ref[...])
        flat = o_ref.reshape(nb * N_MEM * N_XHEADS, LANES)
        for b in range(nb):
            for hd in range(N_XHEADS):
                rows = pl.ds(b * N_MEM * N_XHEADS + hd, N_MEM, stride=N_XHEADS)
                flat[rows, :] = y[b * N_MEM:(b + 1) * N_MEM, hd * LANES:(hd + 1) * LANES]


def _mem_kv(mem, g, w_k, w_v, *, nb):
    n_seq = mem.shape[0]
    assert n_seq % nb == 0 and XHEAD_DIM == 2 * LANES
    rows = nb * N_MEM
    half_cols = N_XHEADS * LANES
    shape = (DEPTH, n_seq, N_MEM, N_XHEADS, XHEAD_DIM)
    w_half = pl.BlockSpec((None, D_MODEL, half_cols), lambda l, part, i: (l, 0, part))
    out_half = pl.BlockSpec((None, nb, N_MEM, N_XHEADS, LANES), lambda l, part, i: (l, i, 0, 0, part))
    return pl.pallas_call(
        functools.partial(_mem_kv_kernel, nb=nb),
        grid=(DEPTH, 2, n_seq // nb),
        in_specs=[pl.BlockSpec((rows, D_MODEL), lambda l, part, i: (i, 0)),
                  pl.BlockSpec((None, 1, D_MODEL), lambda l, part, i: (l, 0, 0)),
                  w_half, w_half],
        out_specs=[out_half] * 2,
        out_shape=[jax.ShapeDtypeStruct(shape, F32)] * 2,
        compiler_params=pltpu.CompilerParams(
            dimension_semantics=("arbitrary", "arbitrary", "arbitrary"),
            vmem_limit_bytes=VMEM_LIMIT_BYTES),
        name="mem_kv",
    )(mem.reshape(n_seq * N_MEM, D_MODEL), g, w_k, w_v)


def _run_trunk(x, pool_st, sconv_st, cconv_st, mem_k, mem_v, layers, g_final, *,
               pos0, mix_nb, mix_lc, attn_nb, attn_lq, ffn_tm, emit_v):
    n_seq, seq_len, _ = x.shape
    stacked = tuple(jnp.zeros_like(s) for s in (pool_st, sconv_st, cconv_st))
    if emit_v:
        stacked += (jnp.zeros((len(layers), n_seq, seq_len, W_BR), F32),)
    for l, lw in enumerate(layers):
        x, stacked = _mixer(x, pool_st, sconv_st, cconv_st, l, stacked, lw,
                            nb=mix_nb, lc=mix_lc, pos0=pos0, emit_v=emit_v)
        x = _attn(x, mem_k, mem_v, l, lw['g_x'], lw['w_xq'], lw['w_xo'],
                  nb=attn_nb, lq=attn_lq)
        x = _ffn(x.reshape(n_seq * seq_len, D_MODEL), lw['g_ffn'], lw['w_ff1'], lw['w_ff2'],
                 g_final, tm=ffn_tm, final_norm=(l == len(layers) - 1))
        x = x.reshape(n_seq, seq_len, D_MODEL)
    return (x,) + stacked


def kernel(x_prompt, x_sample, state_pool, state_sconv, state_cconv, cache_mem_k, cache_mem_v, mem_prompt, norm_mix_g, w_in, pool_w, pool_scale, sconv_w, sgu_ln_g, sgu_ln_b, sgu_w, sgu_b, cconv_w, cconv_b, cconv_ln_g, cconv_ln_b, w_branch, b_gate, w_mix_out, norm_x_g, norm_mem_g, w_xq, w_xk, w_xv, w_xo, norm_ffn_g, w_ff1, w_ff2, norm_f_g):
    batch, seq, _ = x_prompt.shape
    dec_batch, dec_seq, _ = x_sample.shape

    def row(v):
        return v.reshape(1, -1)

    def layer_weights(l, sgu_w_l, sgu_bcol_l):
        return {
            'g_mix': row(norm_mix_g[l]), 'w_in': w_in[l].astype(BF16),
            'pool_w': pool_w[l].astype(BF16), 'pool_scale': row(pool_scale[l]),
            'sconv_w': sconv_w[l], 'sgu_ln_g': row(sgu_ln_g[l]), 'sgu_ln_b': row(sgu_ln_b[l]),
            'sgu_w': sgu_w_l, 'sgu_bcol': sgu_bcol_l,
            'cconv_w': cconv_w[l], 'cconv_b': row(cconv_b[l]),
            'cconv_ln_g': row(cconv_ln_g[l]), 'cconv_ln_b': row(cconv_ln_b[l]),
            'w_branch': w_branch[l].astype(BF16), 'b_gate': b_gate[l],
            'w_mix_out': w_mix_out[l].astype(BF16),
            'g_x': row(norm_x_g[l]), 'w_xq': w_xq[l].astype(BF16), 'w_xo': w_xo[l].astype(BF16),
            'g_ffn': row(norm_ffn_g[l]), 'w_ff1': w_ff1[l].astype(BF16),
            'w_ff2': w_ff2[l].astype(BF16),
        }

    layers_p = [layer_weights(l, sgu_w[l], sgu_b[l].T) for l in range(DEPTH)]
    reps = CHUNK // dec_seq
    layers_s = [dict(layers_p[l],
                     sgu_w=jnp.tile(sgu_w[l][:, :dec_seq, :dec_seq], (1, reps, reps)),
                     sgu_bcol=jnp.tile(sgu_b[l][:, :dec_seq], (1, reps)).T)
                for l in range(DEPTH)]
    g_final = row(norm_f_g)

    def half_head_lane(w):
        w = w.reshape(DEPTH, D_MODEL, N_XHEADS, 2, LANES)
        return jnp.swapaxes(w, 2, 3).reshape(DEPTH, D_MODEL, D_MODEL).astype(BF16)

    mem_k_prompt, mem_v_prompt = _mem_kv(mem_prompt, norm_mem_g.reshape(DEPTH, 1, D_MODEL),
                                         half_head_lane(w_xk), half_head_lane(w_xv), nb=2)
    zeros = lambda rows: jnp.zeros((DEPTH, batch, rows, W_BR), F32)
    y_prompt, pool_p, sconv_p, cconv_p = _run_trunk(
        x_prompt, zeros(POOL_PREV), zeros(SCONV_K - 1), zeros(CCONV_K - 1),
        mem_k_prompt, mem_v_prompt, layers_p, g_final,
        pos0=0, mix_nb=1, mix_lc=256, attn_nb=1, attn_lq=512, ffn_tm=512, emit_v=False)

    y_sample, pool_s, sconv_s, cconv_s, v_s = _run_trunk(
        x_sample, state_pool, state_sconv, state_cconv, cache_mem_k, cache_mem_v,
        layers_s, g_final, pos0=PAST_LEN, mix_nb=256 // dec_seq, mix_lc=dec_seq,
        attn_nb=8, attn_lq=dec_seq, ffn_tm=512, emit_v=True)

    return (y_prompt, y_sample, pool_p, sconv_p, cconv_p, mem_k_prompt, mem_v_prompt,
            pool_s, sconv_s, cconv_s, v_s)
```

```python
import functools

import jax
import jax.numpy as jnp
from jax import lax
from jax.experimental import pallas as pl
from jax.experimental.pallas import tpu as pltpu

D_MODEL = 1024
DEPTH = 2
PAST_LEN = 16384
W_BR = D_MODEL // 2
POOL_WINDOWS = (2, 4, 8, 16)
POOL_GW = W_BR // len(POOL_WINDOWS)
POOL_PREV = max(POOL_WINDOWS) - 1
SCONV_K = 3
CCONV_K = 31
CHUNK = 128
N_SGU_GROUPS = 4
SGU_GW = W_BR // N_SGU_GROUPS
N_MEM = 256
N_XHEADS = 4
XHEAD_DIM = D_MODEL // N_XHEADS
D_FF = 4 * D_MODEL
N_BRANCH = 4
EPS = 1e-6
GATE_COL0 = 8 * W_BR

SUBLANES = 8
LANES = 128
N_PLANES = W_BR // LANES
POOL_OFF = 16
SCONV_OFF = 8
CCONV_OFF = 32

VMEM_LIMIT_BYTES = 56 * 1024 * 1024

BF16 = jnp.bfloat16
F32 = jnp.float32


def _dot(a, b):
    return jnp.dot(a, b, preferred_element_type=F32)


def _rmsnorm(x, g):
    return x * lax.rsqrt(jnp.mean(x * x, axis=-1, keepdims=True) + EPS) * g


def _layernorm(x, g, b):
    xc = x - jnp.mean(x, axis=-1, keepdims=True)
    var = jnp.mean(xc * xc, axis=-1, keepdims=True)
    return xc * lax.rsqrt(var + EPS) * g + b


def _gelu_tanh(x):
    return x * (0.5 * (1.0 + jnp.tanh(0.7978845608028654 * (x + 0.044715 * (x * x * x)))))


def _resident(shape):
    nd = len(shape)
    return pl.BlockSpec(shape, lambda *_: (0,) * nd, pipeline_mode=pl.Buffered(1))


def _layer_resident(stacked, layer):
    rest = (0,) * (stacked.ndim - 1)
    return pl.BlockSpec((None,) + stacked.shape[1:], lambda *_: (layer,) + rest,
                        pipeline_mode=pl.Buffered(1))


def _store_planes(ext, row0, value):
    rows = value.shape[1]
    for c in range(N_PLANES):
        ext[c, :, row0:row0 + rows, :] = value[:, :, c * LANES:(c + 1) * LANES]


def _load_planes(ext, row0, rows):
    return jnp.concatenate([ext[c, :, row0:row0 + rows, :] for c in range(N_PLANES)], axis=-1)


def _causal_dwconv(ext, w_ref, row0, taps, rows):
    planes = []
    for c in range(N_PLANES):
        cs = slice(c * LANES, (c + 1) * LANES)
        acc = w_ref[0:1, cs][None] * ext[c, :, row0:row0 + rows, :]
        for k in range(1, taps):
            acc = acc + w_ref[k:k + 1, cs][None] * ext[c, :, row0 + k:row0 + k + rows, :]
        planes.append(acc)
    return jnp.concatenate(planes, axis=-1)


def _mixer_kernel(x_ref, pool_st_ref, sconv_st_ref, cconv_st_ref,
                  g_mix_ref, w_in_ref, pool_w_ref, pool_scale_ref, sconv_w_ref,
                  sgu_g_ref, sgu_b_ref, sgu_w_ref, sgu_bcol_ref,
                  cconv_w_ref, cconv_b_ref, cln_g_ref, cln_b_ref,
                  w_br_ref, b_gate_ref, w_mix_ref, *rest,
                  nb, lc, nj, pos0, emit_v, n_aliased):
    y_ref, pool_out_ref, sconv_out_ref, cconv_out_ref, *rest = rest[n_aliased:]
    if emit_v:
        v_out_ref, pool_ext, sconv_ext, cconv_ext = rest
    else:
        pool_ext, sconv_ext, cconv_ext = rest
    j = pl.program_id(1)
    tm = nb * lc
    pool_h0 = POOL_OFF - POOL_PREV
    sconv_h0 = SCONV_OFF - (SCONV_K - 1)
    cconv_h0 = CCONV_OFF - (CCONV_K - 1)

    x = x_ref[...].reshape(tm, D_MODEL)
    h = _rmsnorm(x, g_mix_ref[...]).astype(BF16)

    @pl.when(j == 0)
    def _load_history():
        _store_planes(pool_ext, pool_h0, pool_st_ref[...])
        _store_planes(sconv_ext, sconv_h0, sconv_st_ref[...])
        _store_planes(cconv_ext, cconv_h0, cconv_st_ref[...])

    def carry(ext, h0, prev, out_ref):
        new = _load_planes(ext, lc + h0, prev)
        out_ref[...] = new
        if nj > 1:
            _store_planes(ext, h0, new)

    d_in = _dot(h, w_in_ref[:, 6 * W_BR:8 * W_BR])
    c_uv = _dot(h, w_in_ref[:, 4 * W_BR:6 * W_BR])
    z_b = _dot(h, w_in_ref[:, W_BR:4 * W_BR])
    a_u = _dot(h, w_in_ref[:, 0:W_BR])
    gates = [jax.nn.sigmoid(_dot(h, w_in_ref[:, GATE_COL0 + k * D_MODEL:GATE_COL0 + (k + 1) * D_MODEL])
                            + b_gate_ref[k:k + 1, :]) for k in range(N_BRANCH)]

    glu = d_in[:, :W_BR] * jax.nn.sigmoid(d_in[:, W_BR:])
    _store_planes(cconv_ext, CCONV_OFF, glu.reshape(nb, lc, W_BR))
    d_c = _causal_dwconv(cconv_ext, cconv_w_ref, cconv_h0, CCONV_K, lc).reshape(tm, W_BR)
    d_ln = _layernorm(d_c + cconv_b_ref[...], cln_g_ref[...], cln_b_ref[...])
    d_out = d_ln * jax.nn.sigmoid(d_ln)
    carry(cconv_ext, cconv_h0, CCONV_K - 1, cconv_out_ref)

    c_uv = _gelu_tanh(c_uv)
    c_u, c_v = c_uv[:, :W_BR], c_uv[:, W_BR:]
    c_vn = _layernorm(c_v, sgu_g_ref[...], sgu_b_ref[...])
    if emit_v:
        v_out_ref[...] = c_vn.reshape(nb, lc, W_BR)
    row = lax.broadcasted_iota(jnp.int32, (CHUNK, CHUNK), 0)
    col = lax.broadcasted_iota(jnp.int32, (CHUNK, CHUNK), 1)
    keep = row >= col
    seg = min(lc, CHUNK)
    if seg < CHUNK:
        keep = keep & ((row // seg) == (col // seg))
    c_vn_bf = c_vn.astype(BF16)
    s_cols = []
    for g in range(N_SGU_GROUPS):
        sl = slice(g * SGU_GW, (g + 1) * SGU_GW)
        w_g = jnp.where(keep, sgu_w_ref[g], 0.0).astype(BF16)
        bias = sgu_bcol_ref[:, g:g + 1]
        s_rows = [_dot(w_g, c_vn_bf[c * CHUNK:(c + 1) * CHUNK, sl]) + bias
                  for c in range(tm // CHUNK)]
        s_cols.append(jnp.concatenate(s_rows, axis=0))
    c_out = c_u * jnp.concatenate(s_cols, axis=-1)

    b_h, b_b, b_c = z_b[:, :W_BR], z_b[:, W_BR:2 * W_BR], z_b[:, 2 * W_BR:]
    _store_planes(sconv_ext, SCONV_OFF, (b_c * b_h).reshape(nb, lc, W_BR))
    conv = _causal_dwconv(sconv_ext, sconv_w_ref, sconv_h0, SCONV_K, lc)
    b_out = b_b * conv.reshape(tm, W_BR)
    carry(sconv_ext, sconv_h0, SCONV_K - 1, sconv_out_ref)

    _store_planes(pool_ext, POOL_OFF, a_u.reshape(nb, lc, W_BR))
    pos = pos0 + j * lc + lax.broadcasted_iota(jnp.int32, (1, lc, POOL_GW), 1)
    a_parts = []
    for g, w in enumerate(POOL_WINDOWS):
        cur = pool_ext[g, :, POOL_OFF:POOL_OFF + lc, :]
        win = cur
        for i in range(1, w):
            win = win + pool_ext[g, :, POOL_OFF - i:POOL_OFF - i + lc, :]
        cnt = jnp.minimum(pos + 1, w).astype(F32)
        p = win / cnt - cur
        a_parts.append(_dot(p.reshape(tm, POOL_GW).astype(BF16), pool_w_ref[g]))
    a_out = jnp.concatenate(a_parts, axis=-1) * pool_scale_ref[...]
    carry(pool_ext, pool_h0, POOL_PREV, pool_out_ref)

    merged = None
    for k, branch in enumerate((a_out, b_out, c_out, d_out)):
        term = gates[k] * _dot(branch.astype(BF16), w_br_ref[k])
        merged = term if merged is None else merged + term

    y = x + _dot(merged.astype(BF16), w_mix_ref[...])
    y_ref[...] = y.reshape(nb, lc, D_MODEL)


def _mixer(x, pool_st, sconv_st, cconv_st, layer, stacked, lw, *, nb, lc, pos0, emit_v):
    n_seq, seq_len, _ = x.shape
    assert n_seq % nb == 0 and seq_len % lc == 0 and (nb * lc) % CHUNK == 0
    assert lc % SUBLANES == 0 and (lc % CHUNK == 0 or CHUNK % lc == 0)
    assert lc == seq_len or lc >= CCONV_K - 1
    nj = seq_len // lc
    grid = (n_seq // nb, nj)

    def state_tile(rows):
        return pl.BlockSpec((None, nb, rows, W_BR), lambda b, j: (layer, b, 0, 0))

    seq_tile = pl.BlockSpec((nb, lc, D_MODEL), lambda b, j: (b, j, 0))
    params = (lw['g_mix'], lw['w_in'], lw['pool_w'], lw['pool_scale'], lw['sconv_w'],
              lw['sgu_ln_g'], lw['sgu_ln_b'], lw['sgu_w'], lw['sgu_bcol'],
              lw['cconv_w'], lw['cconv_b'], lw['cconv_ln_g'], lw['cconv_ln_b'],
              lw['w_branch'], lw['b_gate'], lw['w_mix_out'])
    in_specs = [seq_tile, state_tile(POOL_PREV), state_tile(SCONV_K - 1), state_tile(CCONV_K - 1)]
    in_specs += [_layer_resident(p, layer) for p in params]
    in_specs += [pl.BlockSpec(memory_space=pl.ANY)] * len(stacked)
    out_shape = [jax.ShapeDtypeStruct(x.shape, F32)]
    out_shape += [jax.ShapeDtypeStruct(s.shape, s.dtype) for s in stacked]
    out_specs = [seq_tile, state_tile(POOL_PREV), state_tile(SCONV_K - 1), state_tile(CCONV_K - 1)]
    if emit_v:
        out_specs.append(pl.BlockSpec((None, nb, lc, W_BR), lambda b, j: (layer, b, j, 0)))
    assert len(stacked) == len(out_specs) - 1
    first_stacked_in = 4 + len(params)
    outs = pl.pallas_call(
        functools.partial(_mixer_kernel, nb=nb, lc=lc, nj=nj, pos0=pos0, emit_v=emit_v,
                          n_aliased=len(stacked)),
        grid=grid, in_specs=in_specs, out_specs=out_specs, out_shape=out_shape,
        input_output_aliases={first_stacked_in + i: 1 + i for i in range(len(stacked))},
        scratch_shapes=[pltpu.VMEM((N_PLANES, nb, POOL_OFF + lc, LANES), F32),
                        pltpu.VMEM((N_PLANES, nb, SCONV_OFF + lc, LANES), F32),
                        pltpu.VMEM((N_PLANES, nb, CCONV_OFF + lc, LANES), F32)],
        compiler_params=pltpu.CompilerParams(
            dimension_semantics=("arbitrary", "arbitrary"),
            vmem_limit_bytes=VMEM_LIMIT_BYTES),
        name="mixer",
    )(x, pool_st, sconv_st, cconv_st, *params, *stacked)
    return outs[0], tuple(outs[1:])


def _attn_kernel(x_ref, klo_ref, khi_ref, vlo_ref, vhi_ref, g_ref, wq_ref, wo_ref, y_ref,
                 k_buf, v_buf, o_buf, *, nb, lq):
    j = pl.program_id(1)
    half = XHEAD_DIM // 2

    @pl.when(j == 0)
    def _gather_heads():
        for lo_ref, hi_ref, buf in ((klo_ref, khi_ref, k_buf), (vlo_ref, vhi_ref, v_buf)):
            lo = lo_ref.reshape(nb * N_MEM * N_XHEADS, half)
            hi = hi_ref.reshape(nb * N_MEM * N_XHEADS, half)
            for b in range(nb):
                for hd in range(N_XHEADS):
                    rows = pl.ds(b * N_MEM * N_XHEADS + hd, N_MEM, stride=N_XHEADS)
                    c0 = hd * XHEAD_DIM
                    buf[b, :, c0:c0 + half] = lo[rows, :].astype(BF16)
                    buf[b, :, c0 + half:c0 + XHEAD_DIM] = hi[rows, :].astype(BF16)

    x = x_ref[...].reshape(nb * lq, D_MODEL)
    h = _rmsnorm(x, g_ref[...]).astype(BF16)
    q = _dot(h, wq_ref[...])
    pairs = [(b, hd) for b in range(nb) for hd in range(N_XHEADS)]
    cols = lambda hd: slice(hd * XHEAD_DIM, (hd + 1) * XHEAD_DIM)
    s = jnp.concatenate(
        [lax.dot_general(q[b * lq:(b + 1) * lq, cols(hd)].astype(BF16), k_buf[b, :, cols(hd)],
                         (((1,), (1,)), ((), ())), preferred_element_type=F32)
         for b, hd in pairs], axis=0) * (XHEAD_DIM ** -0.5)
    e = jnp.exp(s - jnp.max(s, axis=-1, keepdims=True))
    p = e / jnp.sum(e, axis=-1, keepdims=True)
    for i, (b, hd) in enumerate(pairs):
        o_buf[b * lq:(b + 1) * lq, cols(hd)] = _dot(p[i * lq:(i + 1) * lq].astype(BF16),
                                                    v_buf[b, :, cols(hd)])
    y = x + _dot(o_buf[...].astype(BF16), wo_ref[...])
    y_ref[...] = y.reshape(nb, lq, D_MODEL)


def _attn(x, mem_k, mem_v, layer, g, w_q, w_o, *, nb, lq):
    n_seq, seq_len, _ = x.shape
    assert n_seq % nb == 0 and seq_len % lq == 0 and lq % SUBLANES == 0
    assert XHEAD_DIM == 2 * LANES
    seq_tile = pl.BlockSpec((nb, lq, D_MODEL), lambda b, j: (b, j, 0))

    def mem_half(part):
        return pl.BlockSpec((None, nb, N_MEM, N_XHEADS, LANES),
                            lambda b, j: (layer, b, 0, 0, part))

    return pl.pallas_call(
        functools.partial(_attn_kernel, nb=nb, lq=lq),
        grid=(n_seq // nb, seq_len // lq),
        in_specs=[seq_tile, mem_half(0), mem_half(1), mem_half(0), mem_half(1),
                  _layer_resident(g, layer), _layer_resident(w_q, layer),
                  _layer_resident(w_o, layer)],
        out_specs=seq_tile,
        out_shape=jax.ShapeDtypeStruct(x.shape, F32),
        scratch_shapes=[pltpu.VMEM((nb, N_MEM, D_MODEL), BF16),
                        pltpu.VMEM((nb, N_MEM, D_MODEL), BF16),
                        pltpu.VMEM((nb * lq, D_MODEL), F32)],
        compiler_params=pltpu.CompilerParams(
            dimension_semantics=("arbitrary", "arbitrary"),
            vmem_limit_bytes=VMEM_LIMIT_BYTES),
        name="cross_attn",
    )(x, mem_k, mem_k, mem_v, mem_v, g, w_q, w_o)


def _ffn_kernel(x_ref, g_ref, w1_ref, w2_ref, gf_ref, y_ref, *, final_norm):
    x = x_ref[...]
    h = _rmsnorm(x, g_ref[...]).astype(BF16)
    y = x
    for c in range(D_FF // D_MODEL):
        u = jnp.square(jnp.maximum(_dot(h, w1_ref[:, c * D_MODEL:(c + 1) * D_MODEL]), 0.0))
        y = y + _dot(u.astype(BF16), w2_ref[c * D_MODEL:(c + 1) * D_MODEL, :])
    if final_norm:
        y = _rmsnorm(y, gf_ref[...])
    y_ref[...] = y


def _ffn(x, layer, g, w1, w2, g_final, *, tm, final_norm):
    t = x.shape[0]
    assert t % tm == 0
    row_tile = pl.BlockSpec((tm, D_MODEL), lambda i: (i, 0))
    return pl.pallas_call(
        functools.partial(_ffn_kernel, final_norm=final_norm),
        grid=(t // tm,),
        in_specs=[row_tile, _layer_resident(g, layer), _layer_resident(w1, layer),
                  _layer_resident(w2, layer), _resident(g_final.shape)],
        out_specs=row_tile,
        out_shape=jax.ShapeDtypeStruct(x.shape, F32),
        compiler_params=pltpu.CompilerParams(
            dimension_semantics=("arbitrary",), vmem_limit_bytes=VMEM_LIMIT_BYTES),
        name="ffn",
    )(x, g, w1, w2, g_final)


def _mem_kv_kernel(m_ref, g_ref, *refs, nb):
    wk_refs, wv_refs = refs[:N_XHEADS], refs[N_XHEADS:2 * N_XHEADS]
    k_ref, v_ref = refs[2 * N_XHEADS:]
    m = _rmsnorm(m_ref[...], g_ref[...]).astype(BF16)
    for w_refs, o_ref in ((wk_refs, k_ref), (wv_refs, v_ref)):
        y = _dot(m, jnp.concatenate([w[...] for w in w_refs], axis=-1))
        flat = o_ref.reshape(nb * N_MEM * N_XHEADS, LANES)
        for b in range(nb):
            for hd in range(N_XHEADS):
                rows = pl.ds(b * N_MEM * N_XHEADS + hd, N_MEM, stride=N_XHEADS)
                flat[rows, :] = y[b * N_MEM:(b + 1) * N_MEM, hd * LANES:(hd + 1) * LANES]


def _mem_kv(mem, g, w_k, w_v, *, nb):
    n_seq = mem.shape[0]
    assert n_seq % nb == 0 and XHEAD_DIM == 2 * LANES
    rows = nb * N_MEM
    shape = (DEPTH, n_seq, N_MEM, N_XHEADS, XHEAD_DIM)

    def head_half_cols(hd):
        return pl.BlockSpec((None, D_MODEL, LANES), lambda l, part, i: (l, 0, 2 * hd + part))

    w_specs = [head_half_cols(hd) for hd in range(N_XHEADS)]
    out_half = pl.BlockSpec((None, nb, N_MEM, N_XHEADS, LANES), lambda l, part, i: (l, i, 0, 0, part))
    return pl.pallas_call(
        functools.partial(_mem_kv_kernel, nb=nb),
        grid=(DEPTH, 2, n_seq // nb),
        in_specs=[pl.BlockSpec((rows, D_MODEL), lambda l, part, i: (i, 0)),
                  pl.BlockSpec((None, 1, D_MODEL), lambda l, part, i: (l, 0, 0))]
                 + w_specs + w_specs,
        out_specs=[out_half] * 2,
        out_shape=[jax.ShapeDtypeStruct(shape, F32)] * 2,
        compiler_params=pltpu.CompilerParams(
            dimension_semantics=("arbitrary", "arbitrary", "arbitrary"),
            vmem_limit_bytes=VMEM_LIMIT_BYTES),
        name="mem_kv",
    )(mem.reshape(n_seq * N_MEM, D_MODEL), g, *([w_k] * N_XHEADS), *([w_v] * N_XHEADS))


def _run_trunk(x, pool_st, sconv_st, cconv_st, mem_k, mem_v, lw, g_final, *,
               pos0, mix_nb, mix_lc, attn_nb, attn_lq, ffn_tm, emit_v):
    n_seq, seq_len, _ = x.shape
    stacked = tuple(jnp.zeros_like(s) for s in (pool_st, sconv_st, cconv_st))
    if emit_v:
        stacked += (jnp.zeros((DEPTH, n_seq, seq_len, W_BR), F32),)
    for l in range(DEPTH):
        x, stacked = _mixer(x, pool_st, sconv_st, cconv_st, l, stacked, lw,
                            nb=mix_nb, lc=mix_lc, pos0=pos0, emit_v=emit_v)
        x = _attn(x, mem_k, mem_v, l, lw['g_x'], lw['w_xq'], lw['w_xo'],
                  nb=attn_nb, lq=attn_lq)
        x = _ffn(x.reshape(n_seq * seq_len, D_MODEL), l, lw['g_ffn'], lw['w_ff1'], lw['w_ff2'],
                 g_final, tm=ffn_tm, final_norm=(l == DEPTH - 1))
        x = x.reshape(n_seq, seq_len, D_MODEL)
    return (x,) + stacked


def kernel(x_prompt, x_sample, state_pool, state_sconv, state_cconv, cache_mem_k, cache_mem_v, mem_prompt, norm_mix_g, w_in, pool_w, pool_scale, sconv_w, sgu_ln_g, sgu_ln_b, sgu_w, sgu_b, cconv_w, cconv_b, cconv_ln_g, cconv_ln_b, w_branch, b_gate, w_mix_out, norm_x_g, norm_mem_g, w_xq, w_xk, w_xv, w_xo, norm_ffn_g, w_ff1, w_ff2, norm_f_g):
    batch, seq, _ = x_prompt.shape
    dec_batch, dec_seq, _ = x_sample.shape

    def rows(v):
        return v.reshape(DEPTH, 1, -1)

    lw_p = {
        'g_mix': rows(norm_mix_g), 'w_in': w_in.astype(BF16),
        'pool_w': pool_w.astype(BF16), 'pool_scale': rows(pool_scale),
        'sconv_w': sconv_w, 'sgu_ln_g': rows(sgu_ln_g), 'sgu_ln_b': rows(sgu_ln_b),
        'sgu_w': sgu_w, 'sgu_bcol': jnp.swapaxes(sgu_b, 1, 2),
        'cconv_w': cconv_w, 'cconv_b': rows(cconv_b),
        'cconv_ln_g': rows(cconv_ln_g), 'cconv_ln_b': rows(cconv_ln_b),
        'w_branch': w_branch.astype(BF16), 'b_gate': b_gate,
        'w_mix_out': w_mix_out.astype(BF16),
        'g_x': rows(norm_x_g), 'w_xq': w_xq.astype(BF16), 'w_xo': w_xo.astype(BF16),
        'g_ffn': rows(norm_ffn_g), 'w_ff1': w_ff1.astype(BF16), 'w_ff2': w_ff2.astype(BF16),
    }
    reps = CHUNK // dec_seq
    lw_s = dict(lw_p,
                sgu_w=jnp.tile(sgu_w[:, :, :dec_seq, :dec_seq], (1, 1, reps, reps)),
                sgu_bcol=jnp.swapaxes(jnp.tile(sgu_b[:, :, :dec_seq], (1, 1, reps)), 1, 2))
    g_final = norm_f_g.reshape(1, -1)

    mem_k_prompt, mem_v_prompt = _mem_kv(mem_prompt, rows(norm_mem_g),
                                         w_xk.astype(BF16), w_xv.astype(BF16), nb=2)
    zeros = lambda n: jnp.zeros((DEPTH, batch, n, W_BR), F32)
    y_prompt, pool_p, sconv_p, cconv_p = _run_trunk(
        x_prompt, zeros(POOL_PREV), zeros(SCONV_K - 1), zeros(CCONV_K - 1),
        mem_k_prompt, mem_v_prompt, lw_p, g_final,
        pos0=0, mix_nb=1, mix_lc=512, attn_nb=1, attn_lq=512, ffn_tm=512, emit_v=False)

    y_sample, pool_s, sconv_s, cconv_s, v_s = _run_trunk(
        x_sample, state_pool, state_sconv, state_cconv, cache_mem_k, cache_mem_v,
        lw_s, g_final, pos0=PAST_LEN, mix_nb=256 // dec_seq, mix_lc=dec_seq,
        attn_nb=8, attn_lq=dec_seq, ffn_tm=512, emit_v=True)

    return (y_prompt, y_sample, pool_p, sconv_p, cconv_p, mem_k_prompt, mem_v_prompt,
            pool_s, sconv_s, cconv_s, v_s)
```

```python
import functools

import jax
import jax.numpy as jnp
from jax import lax
from jax.experimental import pallas as pl
from jax.experimental.pallas import tpu as pltpu

D_MODEL = 1024
DEPTH = 2
PAST_LEN = 16384
W_BR = D_MODEL // 2
POOL_WINDOWS = (2, 4, 8, 16)
POOL_GW = W_BR // len(POOL_WINDOWS)
POOL_PREV = max(POOL_WINDOWS) - 1
SCONV_K = 3
CCONV_K = 31
CHUNK = 128
N_SGU_GROUPS = 4
SGU_GW = W_BR // N_SGU_GROUPS
N_MEM = 256
N_XHEADS = 4
XHEAD_DIM = D_MODEL // N_XHEADS
D_FF = 4 * D_MODEL
N_BRANCH = 4
EPS = 1e-6
GATE_COL0 = 8 * W_BR

SUBLANES = 8
LANES = 128
N_PLANES = W_BR // LANES
POOL_OFF = 16
SCONV_OFF = 8
CCONV_OFF = 32

VMEM_LIMIT_BYTES = 56 * 1024 * 1024

BF16 = jnp.bfloat16
F32 = jnp.float32


def _dot(a, b):
    return jnp.dot(a, b, preferred_element_type=F32)


def _rmsnorm(x, g):
    return x * lax.rsqrt(jnp.mean(x * x, axis=-1, keepdims=True) + EPS) * g


def _layernorm(x, g, b):
    xc = x - jnp.mean(x, axis=-1, keepdims=True)
    var = jnp.mean(xc * xc, axis=-1, keepdims=True)
    return xc * lax.rsqrt(var + EPS) * g + b


def _gelu_tanh(x):
    return x * (0.5 * (1.0 + jnp.tanh(0.7978845608028654 * (x + 0.044715 * (x * x * x)))))


def _resident(shape):
    nd = len(shape)
    return pl.BlockSpec(shape, lambda *_: (0,) * nd, pipeline_mode=pl.Buffered(1))


def _layer_resident(stacked, layer):
    rest = (0,) * (stacked.ndim - 1)
    return pl.BlockSpec((None,) + stacked.shape[1:], lambda *_: (layer,) + rest,
                        pipeline_mode=pl.Buffered(1))


def _store_planes(ext, row0, value):
    rows = value.shape[1]
    for c in range(N_PLANES):
        ext[c, :, row0:row0 + rows, :] = value[:, :, c * LANES:(c + 1) * LANES]


def _load_planes(ext, row0, rows):
    return jnp.concatenate([ext[c, :, row0:row0 + rows, :] for c in range(N_PLANES)], axis=-1)


def _causal_dwconv(ext, w_ref, row0, taps, rows):
    planes = []
    for c in range(N_PLANES):
        cs = slice(c * LANES, (c + 1) * LANES)
        acc = w_ref[0:1, cs][None] * ext[c, :, row0:row0 + rows, :]
        for k in range(1, taps):
            acc = acc + w_ref[k:k + 1, cs][None] * ext[c, :, row0 + k:row0 + k + rows, :]
        planes.append(acc)
    return jnp.concatenate(planes, axis=-1)


def _mixer_kernel(x_ref, pool_st_ref, sconv_st_ref, cconv_st_ref,
                  g_mix_ref, w_in_ref, pool_w_ref, pool_scale_ref, sconv_w_ref,
                  sgu_g_ref, sgu_b_ref, sgu_w_ref, sgu_bcol_ref,
                  cconv_w_ref, cconv_b_ref, cln_g_ref, cln_b_ref,
                  w_br_ref, b_gate_ref, w_mix_ref, *rest,
                  nb, lc, nj, pos0, emit_v, n_aliased):
    y_ref, pool_out_ref, sconv_out_ref, cconv_out_ref, *rest = rest[n_aliased:]
    if emit_v:
        v_out_ref, pool_ext, sconv_ext, cconv_ext = rest
    else:
        pool_ext, sconv_ext, cconv_ext = rest
    j = pl.program_id(1)
    tm = nb * lc
    pool_h0 = POOL_OFF - POOL_PREV
    sconv_h0 = SCONV_OFF - (SCONV_K - 1)
    cconv_h0 = CCONV_OFF - (CCONV_K - 1)

    x = x_ref[...].reshape(tm, D_MODEL)
    h = _rmsnorm(x, g_mix_ref[...]).astype(BF16)

    @pl.when(j == 0)
    def _load_history():
        _store_planes(pool_ext, pool_h0, pool_st_ref[...])
        _store_planes(sconv_ext, sconv_h0, sconv_st_ref[...])
        _store_planes(cconv_ext, cconv_h0, cconv_st_ref[...])

    def carry(ext, h0, prev, out_ref):
        new = _load_planes(ext, lc + h0, prev)
        out_ref[...] = new
        if nj > 1:
            _store_planes(ext, h0, new)

    d_in = _dot(h, w_in_ref[:, 6 * W_BR:8 * W_BR])
    c_uv = _dot(h, w_in_ref[:, 4 * W_BR:6 * W_BR])
    z_b = _dot(h, w_in_ref[:, W_BR:4 * W_BR])
    a_u = _dot(h, w_in_ref[:, 0:W_BR])
    gates = [jax.nn.sigmoid(_dot(h, w_in_ref[:, GATE_COL0 + k * D_MODEL:GATE_COL0 + (k + 1) * D_MODEL])
                            + b_gate_ref[k:k + 1, :]) for k in range(N_BRANCH)]

    glu = d_in[:, :W_BR] * jax.nn.sigmoid(d_in[:, W_BR:])
    _store_planes(cconv_ext, CCONV_OFF, glu.reshape(nb, lc, W_BR))
    d_c = _causal_dwconv(cconv_ext, cconv_w_ref, cconv_h0, CCONV_K, lc).reshape(tm, W_BR)
    d_ln = _layernorm(d_c + cconv_b_ref[...], cln_g_ref[...], cln_b_ref[...])
    d_out = d_ln * jax.nn.sigmoid(d_ln)
    carry(cconv_ext, cconv_h0, CCONV_K - 1, cconv_out_ref)

    c_uv = _gelu_tanh(c_uv)
    c_u, c_v = c_uv[:, :W_BR], c_uv[:, W_BR:]
    c_vn = _layernorm(c_v, sgu_g_ref[...], sgu_b_ref[...])
    if emit_v:
        v_out_ref[...] = c_vn.reshape(nb, lc, W_BR)
    row = lax.broadcasted_iota(jnp.int32, (CHUNK, CHUNK), 0)
    col = lax.broadcasted_iota(jnp.int32, (CHUNK, CHUNK), 1)
    keep = row >= col
    seg = min(lc, CHUNK)
    if seg < CHUNK:
        keep = keep & ((row // seg) == (col // seg))
    c_vn_bf = c_vn.astype(BF16)
    s_cols = []
    for g in range(N_SGU_GROUPS):
        sl = slice(g * SGU_GW, (g + 1) * SGU_GW)
        w_g = jnp.where(keep, sgu_w_ref[g], 0.0).astype(BF16)
        bias = sgu_bcol_ref[:, g:g + 1]
        s_rows = [_dot(w_g, c_vn_bf[c * CHUNK:(c + 1) * CHUNK, sl]) + bias
                  for c in range(tm // CHUNK)]
        s_cols.append(jnp.concatenate(s_rows, axis=0))
    c_out = c_u * jnp.concatenate(s_cols, axis=-1)

    b_h, b_b, b_c = z_b[:, :W_BR], z_b[:, W_BR:2 * W_BR], z_b[:, 2 * W_BR:]
    _store_planes(sconv_ext, SCONV_OFF, (b_c * b_h).reshape(nb, lc, W_BR))
    conv = _causal_dwconv(sconv_ext, sconv_w_ref, sconv_h0, SCONV_K, lc)
    b_out = b_b * conv.reshape(tm, W_BR)
    carry(sconv_ext, sconv_h0, SCONV_K - 1, sconv_out_ref)

    _store_planes(pool_ext, POOL_OFF, a_u.reshape(nb, lc, W_BR))
    pos = pos0 + j * lc + lax.broadcasted_iota(jnp.int32, (1, lc, POOL_GW), 1)
    a_parts = []
    for g, w in enumerate(POOL_WINDOWS):
        cur = pool_ext[g, :, POOL_OFF:POOL_OFF + lc, :]
        win = cur
        for i in range(1, w):
            win = win + pool_ext[g, :, POOL_OFF - i:POOL_OFF - i + lc, :]
        cnt = jnp.minimum(pos + 1, w).astype(F32)
        p = win / cnt - cur
        a_parts.append(_dot(p.reshape(tm, POOL_GW).astype(BF16), pool_w_ref[g]))
    a_out = jnp.concatenate(a_parts, axis=-1) * pool_scale_ref[...]
    carry(pool_ext, pool_h0, POOL_PREV, pool_out_ref)

    merged = None
    for k, branch in enumerate((a_out, b_out, c_out, d_out)):
        term = gates[k] * _dot(branch.astype(BF16), w_br_ref[k])
        merged = term if merged is None else merged + term

    y = x + _dot(merged.astype(BF16), w_mix_ref[...])
    y_ref[...] = y.reshape(nb, lc, D_MODEL)


def _mixer(x, pool_st, sconv_st, cconv_st, layer, stacked, lw, *, nb, lc, pos0, emit_v):
    n_seq, seq_len, _ = x.shape
    assert n_seq % nb == 0 and seq_len % lc == 0 and (nb * lc) % CHUNK == 0
    assert lc % SUBLANES == 0 and (lc % CHUNK == 0 or CHUNK % lc == 0)
    assert lc == seq_len or lc >= CCONV_K - 1
    nj = seq_len // lc
    grid = (n_seq // nb, nj)

    def state_tile(rows):
        return pl.BlockSpec((None, nb, rows, W_BR), lambda b, j: (layer, b, 0, 0))

    seq_tile = pl.BlockSpec((nb, lc, D_MODEL), lambda b, j: (b, j, 0))
    params = (lw['g_mix'], lw['w_in'], lw['pool_w'], lw['pool_scale'], lw['sconv_w'],
              lw['sgu_ln_g'], lw['sgu_ln_b'], lw['sgu_w'], lw['sgu_bcol'],
              lw['cconv_w'], lw['cconv_b'], lw['cconv_ln_g'], lw['cconv_ln_b'],
              lw['w_branch'], lw['b_gate'], lw['w_mix_out'])
    in_specs = [seq_tile, state_tile(POOL_PREV), state_tile(SCONV_K - 1), state_tile(CCONV_K - 1)]
    in_specs += [_layer_resident(p, layer) for p in params]
    in_specs += [pl.BlockSpec(memory_space=pl.ANY)] * len(stacked)
    out_shape = [jax.ShapeDtypeStruct(x.shape, F32)]
    out_shape += [jax.ShapeDtypeStruct(s.shape, s.dtype) for s in stacked]
    out_specs = [seq_tile, state_tile(POOL_PREV), state_tile(SCONV_K - 1), state_tile(CCONV_K - 1)]
    if emit_v:
        out_specs.append(pl.BlockSpec((None, nb, lc, W_BR), lambda b, j: (layer, b, j, 0)))
    assert len(stacked) == len(out_specs) - 1
    first_stacked_in = 4 + len(params)
    outs = pl.pallas_call(
        functools.partial(_mixer_kernel, nb=nb, lc=lc, nj=nj, pos0=pos0, emit_v=emit_v,
                          n_aliased=len(stacked)),
        grid=grid, in_specs=in_specs, out_specs=out_specs, out_shape=out_shape,
        input_output_aliases={first_stacked_in + i: 1 + i for i in range(len(stacked))},
        scratch_shapes=[pltpu.VMEM((N_PLANES, nb, POOL_OFF + lc, LANES), F32),
                        pltpu.VMEM((N_PLANES, nb, SCONV_OFF + lc, LANES), F32),
                        pltpu.VMEM((N_PLANES, nb, CCONV_OFF + lc, LANES), F32)],
        compiler_params=pltpu.CompilerParams(
            dimension_semantics=("arbitrary", "arbitrary"),
            vmem_limit_bytes=VMEM_LIMIT_BYTES),
        name="mixer",
    )(x, pool_st, sconv_st, cconv_st, *params, *stacked)
    return outs[0], tuple(outs[1:])


def _attn_body(x_ref, klo_ref, khi_ref, vlo_ref, vhi_ref, g_ref, wq_ref, wo_ref, y_ref,
               k_buf, v_buf, o_buf, *, nb, lq, new_memory):
    half = XHEAD_DIM // 2

    def _gather_heads():
        for lo_ref, hi_ref, buf in ((klo_ref, khi_ref, k_buf), (vlo_ref, vhi_ref, v_buf)):
            lo = lo_ref.reshape(nb * N_MEM * N_XHEADS, half)
            hi = hi_ref.reshape(nb * N_MEM * N_XHEADS, half)
            for b in range(nb):
                for hd in range(N_XHEADS):
                    rows = pl.ds(b * N_MEM * N_XHEADS + hd, N_MEM, stride=N_XHEADS)
                    c0 = hd * XHEAD_DIM
                    buf[b, :, c0:c0 + half] = lo[rows, :].astype(BF16)
                    buf[b, :, c0 + half:c0 + XHEAD_DIM] = hi[rows, :].astype(BF16)

    if new_memory is None:
        _gather_heads()
    else:
        pl.when(new_memory)(_gather_heads)

    x = x_ref[...].reshape(nb * lq, D_MODEL)
    h = _rmsnorm(x, g_ref[...]).astype(BF16)
    q = _dot(h, wq_ref[...])
    pairs = [(b, hd) for b in range(nb) for hd in range(N_XHEADS)]
    cols = lambda hd: slice(hd * XHEAD_DIM, (hd + 1) * XHEAD_DIM)
    s = jnp.concatenate(
        [lax.dot_general(q[b * lq:(b + 1) * lq, cols(hd)].astype(BF16), k_buf[b, :, cols(hd)],
                         (((1,), (1,)), ((), ())), preferred_element_type=F32)
         for b, hd in pairs], axis=0) * (XHEAD_DIM ** -0.5)
    e = jnp.exp(s - jnp.max(s, axis=-1, keepdims=True))
    p = e / jnp.sum(e, axis=-1, keepdims=True)
    for i, (b, hd) in enumerate(pairs):
        o_buf[b * lq:(b + 1) * lq, cols(hd)] = _dot(p[i * lq:(i + 1) * lq].astype(BF16),
                                                    v_buf[b, :, cols(hd)])
    y = x + _dot(o_buf[...].astype(BF16), wo_ref[...])
    y_ref[...] = y.reshape(nb, lq, D_MODEL)


def _attn_kernel(*refs, nb, lq):
    _attn_body(*refs, nb=nb, lq=lq, new_memory=pl.program_id(1) == 0)


def _attn(x, mem_k, mem_v, layer, g, w_q, w_o, *, nb, lq):
    n_seq, seq_len, _ = x.shape
    assert n_seq % nb == 0 and seq_len % lq == 0 and lq % SUBLANES == 0
    assert XHEAD_DIM == 2 * LANES
    seq_tile = pl.BlockSpec((nb, lq, D_MODEL), lambda b, j: (b, j, 0))

    def mem_half(part):
        return pl.BlockSpec((None, nb, N_MEM, N_XHEADS, LANES),
                            lambda b, j: (layer, b, 0, 0, part))

    return pl.pallas_call(
        functools.partial(_attn_kernel, nb=nb, lq=lq),
        grid=(n_seq // nb, seq_len // lq),
        in_specs=[seq_tile, mem_half(0), mem_half(1), mem_half(0), mem_half(1),
                  _layer_resident(g, layer), _layer_resident(w_q, layer),
                  _layer_resident(w_o, layer)],
        out_specs=seq_tile,
        out_shape=jax.ShapeDtypeStruct(x.shape, F32),
        scratch_shapes=[pltpu.VMEM((nb, N_MEM, D_MODEL), BF16),
                        pltpu.VMEM((nb, N_MEM, D_MODEL), BF16),
                        pltpu.VMEM((nb * lq, D_MODEL), F32)],
        compiler_params=pltpu.CompilerParams(
            dimension_semantics=("arbitrary", "arbitrary"),
            vmem_limit_bytes=VMEM_LIMIT_BYTES),
        name="cross_attn",
    )(x, mem_k, mem_k, mem_v, mem_v, g, w_q, w_o)


def _ffn_kernel(x_ref, g_ref, w1_ref, w2_ref, gf_ref, y_ref, *, final_norm):
    x = x_ref[...]
    h = _rmsnorm(x, g_ref[...]).astype(BF16)
    y = x
    for c in range(D_FF // D_MODEL):
        u = jnp.square(jnp.maximum(_dot(h, w1_ref[:, c * D_MODEL:(c + 1) * D_MODEL]), 0.0))
        y = y + _dot(u.astype(BF16), w2_ref[c * D_MODEL:(c + 1) * D_MODEL, :])
    if final_norm:
        y = _rmsnorm(y, gf_ref[...])
    y_ref[...] = y


def _ffn(x, layer, g, w1, w2, g_final, *, tm, final_norm):
    t = x.shape[0]
    assert t % tm == 0
    row_tile = pl.BlockSpec((tm, D_MODEL), lambda i: (i, 0))
    return pl.pallas_call(
        functools.partial(_ffn_kernel, final_norm=final_norm),
        grid=(t // tm,),
        in_specs=[row_tile, _layer_resident(g, layer), _layer_resident(w1, layer),
                  _layer_resident(w2, layer), _resident(g_final.shape)],
        out_specs=row_tile,
        out_shape=jax.ShapeDtypeStruct(x.shape, F32),
        compiler_params=pltpu.CompilerParams(
            dimension_semantics=("arbitrary",), vmem_limit_bytes=VMEM_LIMIT_BYTES),
        name="ffn",
    )(x, g, w1, w2, g_final)


def _ffn_and_attn_kernel(xf_ref, gf_ref, w1_ref, w2_ref, gfin_ref,
                         xa_ref, klo_ref, khi_ref, vlo_ref, vhi_ref, ga_ref, wq_ref, wo_ref,
                         yf_ref, ya_ref, k_buf, v_buf, o_buf, *, final_norm, nb, lq):
    _ffn_kernel(xf_ref, gf_ref, w1_ref, w2_ref, gfin_ref, yf_ref, final_norm=final_norm)
    _attn_body(xa_ref, klo_ref, khi_ref, vlo_ref, vhi_ref, ga_ref, wq_ref, wo_ref, ya_ref,
               k_buf, v_buf, o_buf, nb=nb, lq=lq, new_memory=None)


def _ffn_and_attn(x_ffn, x_attn, mem_k, mem_v, layer, lw, g_final, *, tm, final_norm):
    t = x_ffn.shape[0]
    n_seq, lq, _ = x_attn.shape
    assert t % tm == 0 and n_seq % (t // tm) == 0 and lq % SUBLANES == 0
    steps = t // tm
    nb = n_seq // steps
    row_tile = pl.BlockSpec((tm, D_MODEL), lambda i: (i, 0))
    seq_tile = pl.BlockSpec((nb, lq, D_MODEL), lambda i: (i, 0, 0))

    def mem_half(part):
        return pl.BlockSpec((None, nb, N_MEM, N_XHEADS, LANES), lambda i: (layer, i, 0, 0, part))

    return pl.pallas_call(
        functools.partial(_ffn_and_attn_kernel, final_norm=final_norm, nb=nb, lq=lq),
        grid=(steps,),
        in_specs=[row_tile, _layer_resident(lw['g_ffn'], layer), _layer_resident(lw['w_ff1'], layer),
                  _layer_resident(lw['w_ff2'], layer), _resident(g_final.shape),
                  seq_tile, mem_half(0), mem_half(1), mem_half(0), mem_half(1),
                  _layer_resident(lw['g_x'], layer), _layer_resident(lw['w_xq'], layer),
                  _layer_resident(lw['w_xo'], layer)],
        out_specs=[row_tile, seq_tile],
        out_shape=[jax.ShapeDtypeStruct(x_ffn.shape, F32), jax.ShapeDtypeStruct(x_attn.shape, F32)],
        scratch_shapes=[pltpu.VMEM((nb, N_MEM, D_MODEL), BF16),
                        pltpu.VMEM((nb, N_MEM, D_MODEL), BF16),
                        pltpu.VMEM((nb * lq, D_MODEL), F32)],
        compiler_params=pltpu.CompilerParams(
            dimension_semantics=("arbitrary",), vmem_limit_bytes=VMEM_LIMIT_BYTES),
        name="ffn_and_attn",
    )(x_ffn, lw['g_ffn'], lw['w_ff1'], lw['w_ff2'], g_final,
      x_attn, mem_k, mem_k, mem_v, mem_v, lw['g_x'], lw['w_xq'], lw['w_xo'])


def _mem_kv_kernel(m_ref, g_ref, *refs, nb):
    wk_refs, wv_refs = refs[:N_XHEADS], refs[N_XHEADS:2 * N_XHEADS]
    k_ref, v_ref = refs[2 * N_XHEADS:]
    m = _rmsnorm(m_ref[...], g_ref[...]).astype(BF16)
    for w_refs, o_ref in ((wk_refs, k_ref), (wv_refs, v_ref)):
        y = _dot(m, jnp.concatenate([w[...] for w in w_refs], axis=-1))
        flat = o_ref.reshape(nb * N_MEM * N_XHEADS, LANES)
        for b in range(nb):
            for hd in range(N_XHEADS):
                rows = pl.ds(b * N_MEM * N_XHEADS + hd, N_MEM, stride=N_XHEADS)
                flat[rows, :] = y[b * N_MEM:(b + 1) * N_MEM, hd * LANES:(hd + 1) * LANES]


def _mem_kv(mem, g, w_k, w_v, *, nb):
    n_seq = mem.shape[0]
    assert n_seq % nb == 0 and XHEAD_DIM == 2 * LANES
    rows = nb * N_MEM
    shape = (DEPTH, n_seq, N_MEM, N_XHEADS, XHEAD_DIM)

    def head_half_cols(hd):
        return pl.BlockSpec((None, D_MODEL, LANES), lambda l, part, i: (l, 0, 2 * hd + part))

    w_specs = [head_half_cols(hd) for hd in range(N_XHEADS)]
    out_half = pl.BlockSpec((None, nb, N_MEM, N_XHEADS, LANES), lambda l, part, i: (l, i, 0, 0, part))
    return pl.pallas_call(
        functools.partial(_mem_kv_kernel, nb=nb),
        grid=(DEPTH, 2, n_seq // nb),
        in_specs=[pl.BlockSpec((rows, D_MODEL), lambda l, part, i: (i, 0)),
                  pl.BlockSpec((None, 1, D_MODEL), lambda l, part, i: (l, 0, 0))]
                 + w_specs + w_specs,
        out_specs=[out_half] * 2,
        out_shape=[jax.ShapeDtypeStruct(shape, F32)] * 2,
        compiler_params=pltpu.CompilerParams(
            dimension_semantics=("arbitrary", "arbitrary", "arbitrary"),
            vmem_limit_bytes=VMEM_LIMIT_BYTES),
        name="mem_kv",
    )(mem.reshape(n_seq * N_MEM, D_MODEL), g, *([w_k] * N_XHEADS), *([w_v] * N_XHEADS))


def kernel(x_prompt, x_sample, state_pool, state_sconv, state_cconv, cache_mem_k, cache_mem_v, mem_prompt, norm_mix_g, w_in, pool_w, pool_scale, sconv_w, sgu_ln_g, sgu_ln_b, sgu_w, sgu_b, cconv_w, cconv_b, cconv_ln_g, cconv_ln_b, w_branch, b_gate, w_mix_out, norm_x_g, norm_mem_g, w_xq, w_xk, w_xv, w_xo, norm_ffn_g, w_ff1, w_ff2, norm_f_g):
    batch, seq, _ = x_prompt.shape
    dec_batch, dec_seq, _ = x_sample.shape

    def rows(v):
        return v.reshape(DEPTH, 1, -1)

    lw_p = {
        'g_mix': rows(norm_mix_g), 'w_in': w_in.astype(BF16),
        'pool_w': pool_w.astype(BF16), 'pool_scale': rows(pool_scale),
        'sconv_w': sconv_w, 'sgu_ln_g': rows(sgu_ln_g), 'sgu_ln_b': rows(sgu_ln_b),
        'sgu_w': sgu_w, 'sgu_bcol': jnp.swapaxes(sgu_b, 1, 2),
        'cconv_w': cconv_w, 'cconv_b': rows(cconv_b),
        'cconv_ln_g': rows(cconv_ln_g), 'cconv_ln_b': rows(cconv_ln_b),
        'w_branch': w_branch.astype(BF16), 'b_gate': b_gate,
        'w_mix_out': w_mix_out.astype(BF16),
        'g_x': rows(norm_x_g), 'w_xq': w_xq.astype(BF16), 'w_xo': w_xo.astype(BF16),
        'g_ffn': rows(norm_ffn_g), 'w_ff1': w_ff1.astype(BF16), 'w_ff2': w_ff2.astype(BF16),
    }
    reps = CHUNK // dec_seq
    lw_s = dict(lw_p,
                sgu_w=jnp.tile(sgu_w[:, :, :dec_seq, :dec_seq], (1, 1, reps, reps)),
                sgu_bcol=jnp.swapaxes(jnp.tile(sgu_b[:, :, :dec_seq], (1, 1, reps)), 1, 2))
    g_final = norm_f_g.reshape(1, -1)

    mem_k_prompt, mem_v_prompt = _mem_kv(mem_prompt, rows(norm_mem_g),
                                         w_xk.astype(BF16), w_xv.astype(BF16), nb=2)
    zeros = lambda n_seq, n: jnp.zeros((DEPTH, n_seq, n, W_BR), F32)
    hist_p = (zeros(batch, POOL_PREV), zeros(batch, SCONV_K - 1), zeros(batch, CCONV_K - 1))
    hist_s = (state_pool, state_sconv, state_cconv)
    new_p = tuple(jnp.zeros_like(s) for s in hist_p)
    new_s = tuple(jnp.zeros_like(s) for s in hist_s) + (zeros(dec_batch, dec_seq),)
    x_p, x_s = x_prompt, x_sample
    for l in range(DEPTH):
        last = l == DEPTH - 1
        x_p, new_p = _mixer(x_p, *hist_p, l, new_p, lw_p, nb=1, lc=512, pos0=0, emit_v=False)
        x_p = _attn(x_p, mem_k_prompt, mem_v_prompt, l, lw_p['g_x'], lw_p['w_xq'], lw_p['w_xo'],
                    nb=1, lq=512)
        x_s, new_s = _mixer(x_s, *hist_s, l, new_s, lw_s, nb=256 // dec_seq, lc=dec_seq,
                            pos0=PAST_LEN, emit_v=True)
        x_p, x_s = _ffn_and_attn(x_p.reshape(batch * seq, D_MODEL), x_s, cache_mem_k, cache_mem_v,
                                 l, lw_p, g_final, tm=512, final_norm=last)
        x_p = x_p.reshape(batch, seq, D_MODEL)
        x_s = _ffn(x_s.reshape(dec_batch * dec_seq, D_MODEL), l, lw_p['g_ffn'], lw_p['w_ff1'],
                   lw_p['w_ff2'], g_final, tm=512, final_norm=last).reshape(x_sample.shape)

    return (x_p, x_s) + new_p + (mem_k_prompt, mem_v_prompt) + new_s
```

```python
import functools

import jax
import jax.numpy as jnp
from jax import lax
from jax.experimental import pallas as pl
from jax.experimental.pallas import tpu as pltpu

D_MODEL = 1024
DEPTH = 2
PAST_LEN = 16384
W_BR = D_MODEL // 2
POOL_WINDOWS = (2, 4, 8, 16)
POOL_GW = W_BR // len(POOL_WINDOWS)
POOL_PREV = max(POOL_WINDOWS) - 1
SCONV_K = 3
CCONV_K = 31
CHUNK = 128
N_SGU_GROUPS = 4
SGU_GW = W_BR // N_SGU_GROUPS
N_MEM = 256
N_XHEADS = 4
XHEAD_DIM = D_MODEL // N_XHEADS
D_FF = 4 * D_MODEL
N_BRANCH = 4
EPS = 1e-6
GATE_COL0 = 8 * W_BR

SUBLANES = 8
LANES = 128
N_PLANES = W_BR // LANES
POOL_OFF = 16
SCONV_OFF = 8
CCONV_OFF = 32

VMEM_LIMIT_BYTES = 56 * 1024 * 1024

BF16 = jnp.bfloat16
F32 = jnp.float32


def _dot(a, b):
    return jnp.dot(a, b, preferred_element_type=F32)


def _rmsnorm(x, g):
    return x * lax.rsqrt(jnp.mean(x * x, axis=-1, keepdims=True) + EPS) * g


def _layernorm(x, g, b):
    xc = x - jnp.mean(x, axis=-1, keepdims=True)
    var = jnp.mean(xc * xc, axis=-1, keepdims=True)
    return xc * lax.rsqrt(var + EPS) * g + b


def _gelu_tanh(x):
    return x * (0.5 * (1.0 + jnp.tanh(0.7978845608028654 * (x + 0.044715 * (x * x * x)))))


def _resident(shape):
    nd = len(shape)
    return pl.BlockSpec(shape, lambda *_: (0,) * nd, pipeline_mode=pl.Buffered(1))


def _layer_resident(stacked, layer):
    rest = (0,) * (stacked.ndim - 1)
    return pl.BlockSpec((None,) + stacked.shape[1:], lambda *_: (layer,) + rest,
                        pipeline_mode=pl.Buffered(1))


def _store_planes(ext, row0, value):
    rows = value.shape[1]
    for c in range(N_PLANES):
        ext[c, :, row0:row0 + rows, :] = value[:, :, c * LANES:(c + 1) * LANES]


def _load_planes(ext, row0, rows):
    return jnp.concatenate([ext[c, :, row0:row0 + rows, :] for c in range(N_PLANES)], axis=-1)


def _causal_dwconv(ext, w_ref, row0, taps, rows):
    planes = []
    for c in range(N_PLANES):
        cs = slice(c * LANES, (c + 1) * LANES)
        acc = w_ref[0:1, cs][None] * ext[c, :, row0:row0 + rows, :]
        for k in range(1, taps):
            acc = acc + w_ref[k:k + 1, cs][None] * ext[c, :, row0 + k:row0 + k + rows, :]
        planes.append(acc)
    return jnp.concatenate(planes, axis=-1)


def _mixer_kernel(x_ref, pool_st_ref, sconv_st_ref, cconv_st_ref,
                  g_mix_ref, w_in_ref, pool_w_ref, pool_scale_ref, sconv_w_ref,
                  sgu_g_ref, sgu_b_ref, sgu_w_ref, sgu_bcol_ref,
                  cconv_w_ref, cconv_b_ref, cln_g_ref, cln_b_ref,
                  w_br_ref, b_gate_ref, w_mix_ref, *rest,
                  nb, lc, nj, pos0, emit_v, n_aliased):
    y_ref, pool_out_ref, sconv_out_ref, cconv_out_ref, *rest = rest[n_aliased:]
    if emit_v:
        v_out_ref, pool_ext, sconv_ext, cconv_ext = rest
    else:
        pool_ext, sconv_ext, cconv_ext = rest
    j = pl.program_id(1)
    tm = nb * lc
    pool_h0 = POOL_OFF - POOL_PREV
    sconv_h0 = SCONV_OFF - (SCONV_K - 1)
    cconv_h0 = CCONV_OFF - (CCONV_K - 1)

    x = x_ref[...].reshape(tm, D_MODEL)
    h = _rmsnorm(x, g_mix_ref[...]).astype(BF16)

    @pl.when(j == 0)
    def _load_history():
        _store_planes(pool_ext, pool_h0, pool_st_ref[...])
        _store_planes(sconv_ext, sconv_h0, sconv_st_ref[...])
        _store_planes(cconv_ext, cconv_h0, cconv_st_ref[...])

    def carry(ext, h0, prev, out_ref):
        new = _load_planes(ext, lc + h0, prev)
        out_ref[...] = new
        if nj > 1:
            _store_planes(ext, h0, new)

    d_in = _dot(h, w_in_ref[:, 6 * W_BR:8 * W_BR])
    c_uv = _dot(h, w_in_ref[:, 4 * W_BR:6 * W_BR])
    z_b = _dot(h, w_in_ref[:, W_BR:4 * W_BR])
    a_u = _dot(h, w_in_ref[:, 0:W_BR])
    gates = [jax.nn.sigmoid(_dot(h, w_in_ref[:, GATE_COL0 + k * D_MODEL:GATE_COL0 + (k + 1) * D_MODEL])
                            + b_gate_ref[k:k + 1, :]) for k in range(N_BRANCH)]

    glu = d_in[:, :W_BR] * jax.nn.sigmoid(d_in[:, W_BR:])
    _store_planes(cconv_ext, CCONV_OFF, glu.reshape(nb, lc, W_BR))
    d_c = _causal_dwconv(cconv_ext, cconv_w_ref, cconv_h0, CCONV_K, lc).reshape(tm, W_BR)
    d_ln = _layernorm(d_c + cconv_b_ref[...], cln_g_ref[...], cln_b_ref[...])
    d_out = d_ln * jax.nn.sigmoid(d_ln)
    carry(cconv_ext, cconv_h0, CCONV_K - 1, cconv_out_ref)

    c_uv = _gelu_tanh(c_uv)
    c_u, c_v = c_uv[:, :W_BR], c_uv[:, W_BR:]
    c_vn = _layernorm(c_v, sgu_g_ref[...], sgu_b_ref[...])
    if emit_v:
        v_out_ref[...] = c_vn.reshape(nb, lc, W_BR)
    row = lax.broadcasted_iota(jnp.int32, (CHUNK, CHUNK), 0)
    col = lax.broadcasted_iota(jnp.int32, (CHUNK, CHUNK), 1)
    keep = row >= col
    seg = min(lc, CHUNK)
    if seg < CHUNK:
        keep = keep & ((row // seg) == (col // seg))
    c_vn_bf = c_vn.astype(BF16)
    s_cols = []
    for g in range(N_SGU_GROUPS):
        sl = slice(g * SGU_GW, (g + 1) * SGU_GW)
        w_g = jnp.where(keep, sgu_w_ref[g], 0.0).astype(BF16)
        bias = sgu_bcol_ref[:, g:g + 1]
        s_rows = [_dot(w_g, c_vn_bf[c * CHUNK:(c + 1) * CHUNK, sl]) + bias
                  for c in range(tm // CHUNK)]
        s_cols.append(jnp.concatenate(s_rows, axis=0))
    c_out = c_u * jnp.concatenate(s_cols, axis=-1)

    b_h, b_b, b_c = z_b[:, :W_BR], z_b[:, W_BR:2 * W_BR], z_b[:, 2 * W_BR:]
    _store_planes(sconv_ext, SCONV_OFF, (b_c * b_h).reshape(nb, lc, W_BR))
    conv = _causal_dwconv(sconv_ext, sconv_w_ref, sconv_h0, SCONV_K, lc)
    b_out = b_b * conv.reshape(tm, W_BR)
    carry(sconv_ext, sconv_h0, SCONV_K - 1, sconv_out_ref)

    _store_planes(pool_ext, POOL_OFF, a_u.reshape(nb, lc, W_BR))
    pos = pos0 + j * lc + lax.broadcasted_iota(jnp.int32, (1, lc, POOL_GW), 1)
    a_parts = []
    for g, w in enumerate(POOL_WINDOWS):
        cur = pool_ext[g, :, POOL_OFF:POOL_OFF + lc, :]
        win = cur
        for i in range(1, w):
            win = win + pool_ext[g, :, POOL_OFF - i:POOL_OFF - i + lc, :]
        cnt = jnp.minimum(pos + 1, w).astype(F32)
        p = win / cnt - cur
        a_parts.append(_dot(p.reshape(tm, POOL_GW).astype(BF16), pool_w_ref[g]))
    a_out = jnp.concatenate(a_parts, axis=-1) * pool_scale_ref[...]
    carry(pool_ext, pool_h0, POOL_PREV, pool_out_ref)

    merged = None
    for k, branch in enumerate((a_out, b_out, c_out, d_out)):
        term = gates[k] * _dot(branch.astype(BF16), w_br_ref[k])
        merged = term if merged is None else merged + term

    y = x + _dot(merged.astype(BF16), w_mix_ref[...])
    y_ref[...] = y.reshape(nb, lc, D_MODEL)


def _mixer(x, pool_st, sconv_st, cconv_st, layer, stacked, lw, *, nb, lc, pos0, emit_v):
    n_seq, seq_len, _ = x.shape
    assert n_seq % nb == 0 and seq_len % lc == 0 and (nb * lc) % CHUNK == 0
    assert lc % SUBLANES == 0 and (lc % CHUNK == 0 or CHUNK % lc == 0)
    assert lc == seq_len or lc >= CCONV_K - 1
    nj = seq_len // lc
    grid = (n_seq // nb, nj)

    def state_tile(rows):
        return pl.BlockSpec((None, nb, rows, W_BR), lambda b, j: (layer, b, 0, 0))

    seq_tile = pl.BlockSpec((nb, lc, D_MODEL), lambda b, j: (b, j, 0))
    params = (lw['g_mix'], lw['w_in'], lw['pool_w'], lw['pool_scale'], lw['sconv_w'],
              lw['sgu_ln_g'], lw['sgu_ln_b'], lw['sgu_w'], lw['sgu_bcol'],
              lw['cconv_w'], lw['cconv_b'], lw['cconv_ln_g'], lw['cconv_ln_b'],
              lw['w_branch'], lw['b_gate'], lw['w_mix_out'])
    in_specs = [seq_tile, state_tile(POOL_PREV), state_tile(SCONV_K - 1), state_tile(CCONV_K - 1)]
    in_specs += [_layer_resident(p, layer) for p in params]
    in_specs += [pl.BlockSpec(memory_space=pl.ANY)] * len(stacked)
    out_shape = [jax.ShapeDtypeStruct(x.shape, F32)]
    out_shape += [jax.ShapeDtypeStruct(s.shape, s.dtype) for s in stacked]
    out_specs = [seq_tile, state_tile(POOL_PREV), state_tile(SCONV_K - 1), state_tile(CCONV_K - 1)]
    if emit_v:
        out_specs.append(pl.BlockSpec((None, nb, lc, W_BR), lambda b, j: (layer, b, j, 0)))
    assert len(stacked) == len(out_specs) - 1
    first_stacked_in = 4 + len(params)
    outs = pl.pallas_call(
        functools.partial(_mixer_kernel, nb=nb, lc=lc, nj=nj, pos0=pos0, emit_v=emit_v,
                          n_aliased=len(stacked)),
        grid=grid, in_specs=in_specs, out_specs=out_specs, out_shape=out_shape,
        input_output_aliases={first_stacked_in + i: 1 + i for i in range(len(stacked))},
        scratch_shapes=[pltpu.VMEM((N_PLANES, nb, POOL_OFF + lc, LANES), F32),
                        pltpu.VMEM((N_PLANES, nb, SCONV_OFF + lc, LANES), F32),
                        pltpu.VMEM((N_PLANES, nb, CCONV_OFF + lc, LANES), F32)],
        compiler_params=pltpu.CompilerParams(
            dimension_semantics=("arbitrary", "arbitrary"),
            vmem_limit_bytes=VMEM_LIMIT_BYTES),
        name="mixer",
    )(x, pool_st, sconv_st, cconv_st, *params, *stacked)
    return outs[0], tuple(outs[1:])


def _attn_body(x_ref, klo_ref, khi_ref, vlo_ref, vhi_ref, g_ref, wq_ref, wo_ref, y_ref,
               k_buf, v_buf, o_buf, *, nb, lq, new_memory):
    half = XHEAD_DIM // 2

    def _gather_heads():
        for lo_ref, hi_ref, buf in ((klo_ref, khi_ref, k_buf), (vlo_ref, vhi_ref, v_buf)):
            lo = lo_ref.reshape(nb * N_MEM * N_XHEADS, half)
            hi = hi_ref.reshape(nb * N_MEM * N_XHEADS, half)
            for b in range(nb):
                for hd in range(N_XHEADS):
                    rows = pl.ds(b * N_MEM * N_XHEADS + hd, N_MEM, stride=N_XHEADS)
                    c0 = hd * XHEAD_DIM
                    buf[b, :, c0:c0 + half] = lo[rows, :].astype(BF16)
                    buf[b, :, c0 + half:c0 + XHEAD_DIM] = hi[rows, :].astype(BF16)

    if new_memory is None:
        _gather_heads()
    else:
        pl.when(new_memory)(_gather_heads)

    x = x_ref[...].reshape(nb * lq, D_MODEL)
    h = _rmsnorm(x, g_ref[...]).astype(BF16)
    q = _dot(h, wq_ref[...])
    pairs = [(b, hd) for b in range(nb) for hd in range(N_XHEADS)]
    cols = lambda hd: slice(hd * XHEAD_DIM, (hd + 1) * XHEAD_DIM)
    s = jnp.concatenate(
        [lax.dot_general(q[b * lq:(b + 1) * lq, cols(hd)].astype(BF16), k_buf[b, :, cols(hd)],
                         (((1,), (1,)), ((), ())), preferred_element_type=F32)
         for b, hd in pairs], axis=0) * (XHEAD_DIM ** -0.5)
    e = jnp.exp(s - jnp.max(s, axis=-1, keepdims=True))
    p = e / jnp.sum(e, axis=-1, keepdims=True)
    for i, (b, hd) in enumerate(pairs):
        o_buf[b * lq:(b + 1) * lq, cols(hd)] = _dot(p[i * lq:(i + 1) * lq].astype(BF16),
                                                    v_buf[b, :, cols(hd)])
    y = x + _dot(o_buf[...].astype(BF16), wo_ref[...])
    y_ref[...] = y.reshape(nb, lq, D_MODEL)


def _attn_kernel(*refs, nb, lq):
    _attn_body(*refs, nb=nb, lq=lq, new_memory=pl.program_id(1) == 0)


def _attn(x, mem_k, mem_v, layer, g, w_q, w_o, *, nb, lq):
    n_seq, seq_len, _ = x.shape
    assert n_seq % nb == 0 and seq_len % lq == 0 and lq % SUBLANES == 0
    assert XHEAD_DIM == 2 * LANES
    seq_tile = pl.BlockSpec((nb, lq, D_MODEL), lambda b, j: (b, j, 0))

    def mem_half(part):
        return pl.BlockSpec((None, nb, N_MEM, N_XHEADS, LANES),
                            lambda b, j: (layer, b, 0, 0, part))

    return pl.pallas_call(
        functools.partial(_attn_kernel, nb=nb, lq=lq),
        grid=(n_seq // nb, seq_len // lq),
        in_specs=[seq_tile, mem_half(0), mem_half(1), mem_half(0), mem_half(1),
                  _layer_resident(g, layer), _layer_resident(w_q, layer),
                  _layer_resident(w_o, layer)],
        out_specs=seq_tile,
        out_shape=jax.ShapeDtypeStruct(x.shape, F32),
        scratch_shapes=[pltpu.VMEM((nb, N_MEM, D_MODEL), BF16),
                        pltpu.VMEM((nb, N_MEM, D_MODEL), BF16),
                        pltpu.VMEM((nb * lq, D_MODEL), F32)],
        compiler_params=pltpu.CompilerParams(
            dimension_semantics=("arbitrary", "arbitrary"),
            vmem_limit_bytes=VMEM_LIMIT_BYTES),
        name="cross_attn",
    )(x, mem_k, mem_k, mem_v, mem_v, g, w_q, w_o)


def _ffn_kernel(x_ref, g_ref, w1_ref, w2_ref, gf_ref, y_ref, *, final_norm):
    x = x_ref[...]
    h = _rmsnorm(x, g_ref[...]).astype(BF16)
    y = x
    for c in range(D_FF // D_MODEL):
        u = jnp.square(jnp.maximum(_dot(h, w1_ref[:, c * D_MODEL:(c + 1) * D_MODEL]), 0.0))
        y = y + _dot(u.astype(BF16), w2_ref[c * D_MODEL:(c + 1) * D_MODEL, :])
    if final_norm:
        y = _rmsnorm(y, gf_ref[...])
    y_ref[...] = y


def _ffn(x, layer, g, w1, w2, g_final, *, tm, final_norm):
    t = x.shape[0]
    assert t % tm == 0
    row_tile = pl.BlockSpec((tm, D_MODEL), lambda i: (i, 0))
    return pl.pallas_call(
        functools.partial(_ffn_kernel, final_norm=final_norm),
        grid=(t // tm,),
        in_specs=[row_tile, _layer_resident(g, layer), _layer_resident(w1, layer),
                  _layer_resident(w2, layer), _resident(g_final.shape)],
        out_specs=row_tile,
        out_shape=jax.ShapeDtypeStruct(x.shape, F32),
        compiler_params=pltpu.CompilerParams(
            dimension_semantics=("arbitrary",), vmem_limit_bytes=VMEM_LIMIT_BYTES),
        name="ffn",
    )(x, g, w1, w2, g_final)


def _ffn_and_attn_kernel(xf_ref, gf_ref, w1_ref, w2_ref, gfin_ref,
                         xa_ref, klo_ref, khi_ref, vlo_ref, vhi_ref, ga_ref, wq_ref, wo_ref,
                         yf_ref, ya_ref, k_buf, v_buf, o_buf, *, final_norm, nb, lq):
    _ffn_kernel(xf_ref, gf_ref, w1_ref, w2_ref, gfin_ref, yf_ref, final_norm=final_norm)
    _attn_body(xa_ref, klo_ref, khi_ref, vlo_ref, vhi_ref, ga_ref, wq_ref, wo_ref, ya_ref,
               k_buf, v_buf, o_buf, nb=nb, lq=lq, new_memory=None)


def _ffn_and_attn(x_ffn, x_attn, mem_k, mem_v, layer, lw, g_final, *, tm, final_norm):
    t = x_ffn.shape[0]
    n_seq, lq, _ = x_attn.shape
    assert t % tm == 0 and n_seq % (t // tm) == 0 and lq % SUBLANES == 0
    steps = t // tm
    nb = n_seq // steps
    row_tile = pl.BlockSpec((tm, D_MODEL), lambda i: (i, 0))
    seq_tile = pl.BlockSpec((nb, lq, D_MODEL), lambda i: (i, 0, 0))

    def mem_half(part):
        return pl.BlockSpec((None, nb, N_MEM, N_XHEADS, LANES), lambda i: (layer, i, 0, 0, part))

    return pl.pallas_call(
        functools.partial(_ffn_and_attn_kernel, final_norm=final_norm, nb=nb, lq=lq),
        grid=(steps,),
        in_specs=[row_tile, _layer_resident(lw['g_ffn'], layer), _layer_resident(lw['w_ff1'], layer),
                  _layer_resident(lw['w_ff2'], layer), _resident(g_final.shape),
                  seq_tile, mem_half(0), mem_half(1), mem_half(0), mem_half(1),
                  _layer_resident(lw['g_x'], layer), _layer_resident(lw['w_xq'], layer),
                  _layer_resident(lw['w_xo'], layer)],
        out_specs=[row_tile, seq_tile],
        out_shape=[jax.ShapeDtypeStruct(x_ffn.shape, F32), jax.ShapeDtypeStruct(x_attn.shape, F32)],
        scratch_shapes=[pltpu.VMEM((nb, N_MEM, D_MODEL), BF16),
                        pltpu.VMEM((nb, N_MEM, D_MODEL), BF16),
                        pltpu.VMEM((nb * lq, D_MODEL), F32)],
        compiler_params=pltpu.CompilerParams(
            dimension_semantics=("arbitrary",), vmem_limit_bytes=VMEM_LIMIT_BYTES),
        name="ffn_and_attn",
    )(x_ffn, lw['g_ffn'], lw['w_ff1'], lw['w_ff2'], g_final,
      x_attn, mem_k, mem_k, mem_v, mem_v, lw['g_x'], lw['w_xq'], lw['w_xo'])


def _mem_kv_kernel(m_ref, g_ref, *refs, nb):
    wk_refs, wv_refs = refs[:N_XHEADS], refs[N_XHEADS:2 * N_XHEADS]
    k_ref, v_ref = refs[2 * N_XHEADS:]
    m = _rmsnorm(m_ref[...], g_ref[...]).astype(BF16)
    for w_refs, o_ref in ((wk_refs, k_ref), (wv_refs, v_ref)):
        cols = jnp.concatenate([r[...].astype(BF16) for r in w_refs], axis=-1)
        y = _dot(m, cols)
        flat = o_ref.reshape(nb * N_MEM * N_XHEADS, LANES)
        for b in range(nb):
            for hd in range(N_XHEADS):
                rows = pl.ds(b * N_MEM * N_XHEADS + hd, N_MEM, stride=N_XHEADS)
                flat[rows, :] = y[b * N_MEM:(b + 1) * N_MEM, hd * LANES:(hd + 1) * LANES]


def _mem_kv(mem, g, w_k, w_v, *, nb):
    n_seq = mem.shape[0]
    assert n_seq % nb == 0 and XHEAD_DIM == 2 * LANES
    rows = nb * N_MEM
    shape = (DEPTH, n_seq, N_MEM, N_XHEADS, XHEAD_DIM)

    def head_half_cols(hd):
        return pl.BlockSpec((None, D_MODEL, LANES), lambda l, part, i: (l, 0, 2 * hd + part))

    w_specs = [head_half_cols(hd) for hd in range(N_XHEADS)]
    out_half = pl.BlockSpec((None, nb, N_MEM, N_XHEADS, LANES), lambda l, part, i: (l, i, 0, 0, part))
    return pl.pallas_call(
        functools.partial(_mem_kv_kernel, nb=nb),
        grid=(DEPTH, 2, n_seq // nb),
        in_specs=[pl.BlockSpec((rows, D_MODEL), lambda l, part, i: (i, 0)),
                  pl.BlockSpec((None, 1, D_MODEL), lambda l, part, i: (l, 0, 0))]
                 + w_specs + w_specs,
        out_specs=[out_half] * 2,
        out_shape=[jax.ShapeDtypeStruct(shape, F32)] * 2,
        compiler_params=pltpu.CompilerParams(
            dimension_semantics=("arbitrary", "arbitrary", "arbitrary"),
            vmem_limit_bytes=VMEM_LIMIT_BYTES),
        name="mem_kv",
    )(mem.reshape(n_seq * N_MEM, D_MODEL), g, *([w_k] * N_XHEADS), *([w_v] * N_XHEADS))


def kernel(x_prompt, x_sample, state_pool, state_sconv, state_cconv, cache_mem_k, cache_mem_v, mem_prompt, norm_mix_g, w_in, pool_w, pool_scale, sconv_w, sgu_ln_g, sgu_ln_b, sgu_w, sgu_b, cconv_w, cconv_b, cconv_ln_g, cconv_ln_b, w_branch, b_gate, w_mix_out, norm_x_g, norm_mem_g, w_xq, w_xk, w_xv, w_xo, norm_ffn_g, w_ff1, w_ff2, norm_f_g):
    batch, seq, _ = x_prompt.shape
    dec_batch, dec_seq, _ = x_sample.shape

    def rows(v):
        return v.reshape(DEPTH, 1, -1)

    lw_p = {
        'g_mix': rows(norm_mix_g), 'w_in': w_in.astype(BF16),
        'pool_w': pool_w.astype(BF16), 'pool_scale': rows(pool_scale),
        'sconv_w': sconv_w, 'sgu_ln_g': rows(sgu_ln_g), 'sgu_ln_b': rows(sgu_ln_b),
        'sgu_w': sgu_w, 'sgu_bcol': jnp.swapaxes(sgu_b, 1, 2),
        'cconv_w': cconv_w, 'cconv_b': rows(cconv_b),
        'cconv_ln_g': rows(cconv_ln_g), 'cconv_ln_b': rows(cconv_ln_b),
        'w_branch': w_branch.astype(BF16), 'b_gate': b_gate,
        'w_mix_out': w_mix_out.astype(BF16),
        'g_x': rows(norm_x_g), 'w_xq': w_xq.astype(BF16), 'w_xo': w_xo.astype(BF16),
        'g_ffn': rows(norm_ffn_g), 'w_ff1': w_ff1.astype(BF16), 'w_ff2': w_ff2.astype(BF16),
    }
    reps = CHUNK // dec_seq
    lw_s = dict(lw_p,
                sgu_w=jnp.tile(sgu_w[:, :, :dec_seq, :dec_seq], (1, 1, reps, reps)),
                sgu_bcol=jnp.swapaxes(jnp.tile(sgu_b[:, :, :dec_seq], (1, 1, reps)), 1, 2))
    g_final = norm_f_g.reshape(1, -1)

    mem_k_prompt, mem_v_prompt = _mem_kv(mem_prompt, rows(norm_mem_g), w_xk, w_xv, nb=4)
    zeros = lambda n_seq, n: jnp.zeros((DEPTH, n_seq, n, W_BR), F32)
    hist_p = (zeros(batch, POOL_PREV), zeros(batch, SCONV_K - 1), zeros(batch, CCONV_K - 1))
    hist_s = (state_pool, state_sconv, state_cconv)
    new_p = tuple(jnp.zeros_like(s) for s in hist_p)
    new_s = tuple(jnp.zeros_like(s) for s in hist_s) + (zeros(dec_batch, dec_seq),)
    x_p, x_s = x_prompt, x_sample
    for l in range(DEPTH):
        last = l == DEPTH - 1
        x_p, new_p = _mixer(x_p, *hist_p, l, new_p, lw_p, nb=1, lc=512, pos0=0, emit_v=False)
        x_p = _attn(x_p, mem_k_prompt, mem_v_prompt, l, lw_p['g_x'], lw_p['w_xq'], lw_p['w_xo'],
                    nb=1, lq=1024)
        x_s, new_s = _mixer(x_s, *hist_s, l, new_s, lw_s, nb=256 // dec_seq, lc=dec_seq,
                            pos0=PAST_LEN, emit_v=True)
        x_p, x_s = _ffn_and_attn(x_p.reshape(batch * seq, D_MODEL), x_s, cache_mem_k, cache_mem_v,
                                 l, lw_p, g_final, tm=512, final_norm=last)
        x_p = x_p.reshape(batch, seq, D_MODEL)
        x_s = _ffn(x_s.reshape(dec_batch * dec_seq, D_MODEL), l, lw_p['g_ffn'], lw_p['w_ff1'],
                   lw_p['w_ff2'], g_final, tm=512, final_norm=last).reshape(x_sample.shape)

    return (x_p, x_s) + new_p + (mem_k_prompt, mem_v_prompt) + new_s
```

```python
import functools

import jax
import jax.numpy as jnp
from jax import lax
from jax.experimental import pallas as pl
from jax.experimental.pallas import tpu as pltpu

D_MODEL = 1024
DEPTH = 2
PAST_LEN = 16384
W_BR = D_MODEL // 2
POOL_WINDOWS = (2, 4, 8, 16)
POOL_GW = W_BR // len(POOL_WINDOWS)
POOL_PREV = max(POOL_WINDOWS) - 1
SCONV_K = 3
CCONV_K = 31
CHUNK = 128
N_SGU_GROUPS = 4
SGU_GW = W_BR // N_SGU_GROUPS
N_MEM = 256
N_XHEADS = 4
XHEAD_DIM = D_MODEL // N_XHEADS
D_FF = 4 * D_MODEL
N_BRANCH = 4
EPS = 1e-6
GATE_COL0 = 8 * W_BR

SUBLANES = 8
LANES = 128
N_PLANES = W_BR // LANES
POOL_OFF = 16
SCONV_OFF = 8
CCONV_OFF = 32

VMEM_LIMIT_BYTES = 56 * 1024 * 1024

BF16 = jnp.bfloat16
F32 = jnp.float32


def _dot(a, b):
    return jnp.dot(a, b, preferred_element_type=F32)


def _rmsnorm(x, g):
    return x * lax.rsqrt(jnp.mean(x * x, axis=-1, keepdims=True) + EPS) * g


def _layernorm(x, g, b):
    xc = x - jnp.mean(x, axis=-1, keepdims=True)
    var = jnp.mean(xc * xc, axis=-1, keepdims=True)
    return xc * lax.rsqrt(var + EPS) * g + b


def _gelu_tanh(x):
    return x * (0.5 * (1.0 + jnp.tanh(0.7978845608028654 * (x + 0.044715 * (x * x * x)))))


def _resident(shape):
    nd = len(shape)
    return pl.BlockSpec(shape, lambda *_: (0,) * nd, pipeline_mode=pl.Buffered(1))


def _layer_resident(stacked, layer):
    rest = (0,) * (stacked.ndim - 1)
    return pl.BlockSpec((None,) + stacked.shape[1:], lambda *_: (layer,) + rest,
                        pipeline_mode=pl.Buffered(1))


def _store_planes(ext, row0, value):
    rows = value.shape[1]
    for c in range(N_PLANES):
        ext[c, :, row0:row0 + rows, :] = value[:, :, c * LANES:(c + 1) * LANES]


def _load_planes(ext, row0, rows):
    return jnp.concatenate([ext[c, :, row0:row0 + rows, :] for c in range(N_PLANES)], axis=-1)


def _causal_dwconv(ext, w_ref, row0, taps, rows):
    planes = []
    for c in range(N_PLANES):
        cs = slice(c * LANES, (c + 1) * LANES)
        acc = w_ref[0:1, cs][None] * ext[c, :, row0:row0 + rows, :]
        for k in range(1, taps):
            acc = acc + w_ref[k:k + 1, cs][None] * ext[c, :, row0 + k:row0 + k + rows, :]
        planes.append(acc)
    return jnp.concatenate(planes, axis=-1)


def _mixer_kernel(x_ref, pool_st_ref, sconv_st_ref, cconv_st_ref,
                  g_mix_ref, w_in_ref, pool_w_ref, pool_scale_ref, sconv_w_ref,
                  sgu_g_ref, sgu_b_ref, sgu_w_ref, sgu_bcol_ref,
                  cconv_w_ref, cconv_b_ref, cln_g_ref, cln_b_ref,
                  w_br_ref, b_gate_ref, w_mix_ref, *rest,
                  nb, lc, nj, pos0, emit_v, n_aliased):
    y_ref, pool_out_ref, sconv_out_ref, cconv_out_ref, *rest = rest[n_aliased:]
    if emit_v:
        v_out_ref, pool_ext, sconv_ext, cconv_ext = rest
    else:
        pool_ext, sconv_ext, cconv_ext = rest
    j = pl.program_id(1)
    tm = nb * lc
    pool_h0 = POOL_OFF - POOL_PREV
    sconv_h0 = SCONV_OFF - (SCONV_K - 1)
    cconv_h0 = CCONV_OFF - (CCONV_K - 1)

    x = x_ref[...].reshape(tm, D_MODEL)
    h = _rmsnorm(x, g_mix_ref[...]).astype(BF16)

    @pl.when(j == 0)
    def _load_history():
        _store_planes(pool_ext, pool_h0, pool_st_ref[...])
        _store_planes(sconv_ext, sconv_h0, sconv_st_ref[...])
        _store_planes(cconv_ext, cconv_h0, cconv_st_ref[...])

    def carry(ext, h0, prev, out_ref):
        new = _load_planes(ext, lc + h0, prev)
        out_ref[...] = new
        if nj > 1:
            _store_planes(ext, h0, new)

    d_in = _dot(h, w_in_ref[:, 6 * W_BR:8 * W_BR])
    c_uv = _dot(h, w_in_ref[:, 4 * W_BR:6 * W_BR])
    z_b = _dot(h, w_in_ref[:, W_BR:4 * W_BR])
    a_u = _dot(h, w_in_ref[:, 0:W_BR])
    gates = [jax.nn.sigmoid(_dot(h, w_in_ref[:, GATE_COL0 + k * D_MODEL:GATE_COL0 + (k + 1) * D_MODEL])
                            + b_gate_ref[k:k + 1, :]) for k in range(N_BRANCH)]

    glu = d_in[:, :W_BR] * jax.nn.sigmoid(d_in[:, W_BR:])
    _store_planes(cconv_ext, CCONV_OFF, glu.reshape(nb, lc, W_BR))
    d_c = _causal_dwconv(cconv_ext, cconv_w_ref, cconv_h0, CCONV_K, lc).reshape(tm, W_BR)
    d_ln = _layernorm(d_c + cconv_b_ref[...], cln_g_ref[...], cln_b_ref[...])
    d_out = d_ln * jax.nn.sigmoid(d_ln)
    carry(cconv_ext, cconv_h0, CCONV_K - 1, cconv_out_ref)

    c_uv = _gelu_tanh(c_uv)
    c_u, c_v = c_uv[:, :W_BR], c_uv[:, W_BR:]
    c_vn = _layernorm(c_v, sgu_g_ref[...], sgu_b_ref[...])
    if emit_v:
        v_out_ref[...] = c_vn.reshape(nb, lc, W_BR)
    row = lax.broadcasted_iota(jnp.int32, (CHUNK, CHUNK), 0)
    col = lax.broadcasted_iota(jnp.int32, (CHUNK, CHUNK), 1)
    keep = row >= col
    seg = min(lc, CHUNK)
    if seg < CHUNK:
        keep = keep & ((row // seg) == (col // seg))
    c_vn_bf = c_vn.astype(BF16)
    s_cols = []
    for g in range(N_SGU_GROUPS):
        sl = slice(g * SGU_GW, (g + 1) * SGU_GW)
        w_g = jnp.where(keep, sgu_w_ref[g], 0.0).astype(BF16)
        bias = sgu_bcol_ref[:, g:g + 1]
        s_rows = [_dot(w_g, c_vn_bf[c * CHUNK:(c + 1) * CHUNK, sl]) + bias
                  for c in range(tm // CHUNK)]
        s_cols.append(jnp.concatenate(s_rows, axis=0))
    c_out = c_u * jnp.concatenate(s_cols, axis=-1)

    b_h, b_b, b_c = z_b[:, :W_BR], z_b[:, W_BR:2 * W_BR], z_b[:, 2 * W_BR:]
    _store_planes(sconv_ext, SCONV_OFF, (b_c * b_h).reshape(nb, lc, W_BR))
    conv = _causal_dwconv(sconv_ext, sconv_w_ref, sconv_h0, SCONV_K, lc)
    b_out = b_b * conv.reshape(tm, W_BR)
    carry(sconv_ext, sconv_h0, SCONV_K - 1, sconv_out_ref)

    _store_planes(pool_ext, POOL_OFF, a_u.reshape(nb, lc, W_BR))
    pos = pos0 + j * lc + lax.broadcasted_iota(jnp.int32, (1, lc, POOL_GW), 1)
    a_parts = []
    for g, w in enumerate(POOL_WINDOWS):
        cur = pool_ext[g, :, POOL_OFF:POOL_OFF + lc, :]
        win = cur
        for i in range(1, w):
            win = win + pool_ext[g, :, POOL_OFF - i:POOL_OFF - i + lc, :]
        cnt = jnp.minimum(pos + 1, w).astype(F32)
        p = win / cnt - cur
        a_parts.append(_dot(p.reshape(tm, POOL_GW).astype(BF16), pool_w_ref[g]))
    a_out = jnp.concatenate(a_parts, axis=-1) * pool_scale_ref[...]
    carry(pool_ext, pool_h0, POOL_PREV, pool_out_ref)

    merged = None
    for k, branch in enumerate((a_out, b_out, c_out, d_out)):
        term = gates[k] * _dot(branch.astype(BF16), w_br_ref[k])
        merged = term if merged is None else merged + term

    y = x + _dot(merged.astype(BF16), w_mix_ref[...])
    y_ref[...] = y.reshape(nb, lc, D_MODEL)


def _mixer(x, pool_st, sconv_st, cconv_st, layer, stacked, lw, *, nb, lc, pos0, emit_v):
    n_seq, seq_len, _ = x.shape
    assert n_seq % nb == 0 and seq_len % lc == 0 and (nb * lc) % CHUNK == 0
    assert lc % SUBLANES == 0 and (lc % CHUNK == 0 or CHUNK % lc == 0)
    assert lc == seq_len or lc >= CCONV_K - 1
    nj = seq_len // lc
    grid = (n_seq // nb, nj)

    def state_tile(rows):
        return pl.BlockSpec((None, nb, rows, W_BR), lambda b, j: (layer, b, 0, 0))

    seq_tile = pl.BlockSpec((nb, lc, D_MODEL), lambda b, j: (b, j, 0))
    params = (lw['g_mix'], lw['w_in'], lw['pool_w'], lw['pool_scale'], lw['sconv_w'],
              lw['sgu_ln_g'], lw['sgu_ln_b'], lw['sgu_w'], lw['sgu_bcol'],
              lw['cconv_w'], lw['cconv_b'], lw['cconv_ln_g'], lw['cconv_ln_b'],
              lw['w_branch'], lw['b_gate'], lw['w_mix_out'])
    in_specs = [seq_tile, state_tile(POOL_PREV), state_tile(SCONV_K - 1), state_tile(CCONV_K - 1)]
    in_specs += [_layer_resident(p, layer) for p in params]
    in_specs += [pl.BlockSpec(memory_space=pl.ANY)] * len(stacked)
    out_shape = [jax.ShapeDtypeStruct(x.shape, F32)]
    out_shape += [jax.ShapeDtypeStruct(s.shape, s.dtype) for s in stacked]
    out_specs = [seq_tile, state_tile(POOL_PREV), state_tile(SCONV_K - 1), state_tile(CCONV_K - 1)]
    if emit_v:
        out_specs.append(pl.BlockSpec((None, nb, lc, W_BR), lambda b, j: (layer, b, j, 0)))
    assert len(stacked) == len(out_specs) - 1
    first_stacked_in = 4 + len(params)
    outs = pl.pallas_call(
        functools.partial(_mixer_kernel, nb=nb, lc=lc, nj=nj, pos0=pos0, emit_v=emit_v,
                          n_aliased=len(stacked)),
        grid=grid, in_specs=in_specs, out_specs=out_specs, out_shape=out_shape,
        input_output_aliases={first_stacked_in + i: 1 + i for i in range(len(stacked))},
        scratch_shapes=[pltpu.VMEM((N_PLANES, nb, POOL_OFF + lc, LANES), F32),
                        pltpu.VMEM((N_PLANES, nb, SCONV_OFF + lc, LANES), F32),
                        pltpu.VMEM((N_PLANES, nb, CCONV_OFF + lc, LANES), F32)],
        compiler_params=pltpu.CompilerParams(
            dimension_semantics=("arbitrary", "arbitrary"),
            vmem_limit_bytes=VMEM_LIMIT_BYTES),
        name="mixer",
    )(x, pool_st, sconv_st, cconv_st, *params, *stacked)
    return outs[0], tuple(outs[1:])


def _attn_body(x_ref, klo_ref, khi_ref, vlo_ref, vhi_ref, g_ref, wq_ref, wo_ref, y_ref,
               k_buf, v_buf, o_buf, *, nb, lq, new_memory):
    half = XHEAD_DIM // 2

    def _gather_heads():
        for lo_ref, hi_ref, buf in ((klo_ref, khi_ref, k_buf), (vlo_ref, vhi_ref, v_buf)):
            lo = lo_ref.reshape(nb * N_MEM * N_XHEADS, half)
            hi = hi_ref.reshape(nb * N_MEM * N_XHEADS, half)
            for b in range(nb):
                for hd in range(N_XHEADS):
                    rows = pl.ds(b * N_MEM * N_XHEADS + hd, N_MEM, stride=N_XHEADS)
                    c0 = hd * XHEAD_DIM
                    buf[b, :, c0:c0 + half] = lo[rows, :].astype(BF16)
                    buf[b, :, c0 + half:c0 + XHEAD_DIM] = hi[rows, :].astype(BF16)

    if new_memory is None:
        _gather_heads()
    else:
        pl.when(new_memory)(_gather_heads)

    x = x_ref[...].reshape(nb * lq, D_MODEL)
    h = _rmsnorm(x, g_ref[...]).astype(BF16)
    q = _dot(h, wq_ref[...])
    pairs = [(b, hd) for b in range(nb) for hd in range(N_XHEADS)]
    cols = lambda hd: slice(hd * XHEAD_DIM, (hd + 1) * XHEAD_DIM)
    s = jnp.concatenate(
        [lax.dot_general(q[b * lq:(b + 1) * lq, cols(hd)].astype(BF16), k_buf[b, :, cols(hd)],
                         (((1,), (1,)), ((), ())), preferred_element_type=F32)
         for b, hd in pairs], axis=0) * (XHEAD_DIM ** -0.5)
    e = jnp.exp(s - jnp.max(s, axis=-1, keepdims=True))
    p = e / jnp.sum(e, axis=-1, keepdims=True)
    for i, (b, hd) in enumerate(pairs):
        o_buf[b * lq:(b + 1) * lq, cols(hd)] = _dot(p[i * lq:(i + 1) * lq].astype(BF16),
                                                    v_buf[b, :, cols(hd)])
    y = x + _dot(o_buf[...].astype(BF16), wo_ref[...])
    y_ref[...] = y.reshape(nb, lq, D_MODEL)


N_ATTN_INPUTS = 8


def _attn_kernel(*refs, nb, lq, n_cast):
    ins, refs = refs[:N_ATTN_INPUTS], refs[N_ATTN_INPUTS:]
    cast_in, refs = refs[:n_cast], refs[n_cast:]
    y_ref, refs = refs[0], refs[1:]
    cast_out, scratch = refs[:n_cast], refs[n_cast:]
    _attn_body(*ins, y_ref, *scratch, nb=nb, lq=lq, new_memory=pl.program_id(1) == 0)
    for src_ref, dst_ref in zip(cast_in, cast_out):
        dst_ref[...] = src_ref[...].astype(BF16)


def _attn(x, mem_k, mem_v, layer, g, w_q, w_o, *, nb, lq, also_cast=()):
    n_seq, seq_len, _ = x.shape
    assert n_seq % nb == 0 and seq_len % lq == 0 and lq % SUBLANES == 0
    assert XHEAD_DIM == 2 * LANES
    nj = seq_len // lq
    steps = (n_seq // nb) * nj
    seq_tile = pl.BlockSpec((nb, lq, D_MODEL), lambda b, j: (b, j, 0))

    def mem_half(part):
        return pl.BlockSpec((None, nb, N_MEM, N_XHEADS, LANES),
                            lambda b, j: (layer, b, 0, 0, part))

    def row_block(a):
        assert a.ndim == 2 and a.shape[0] % (steps * 2 * SUBLANES) == 0
        return pl.BlockSpec((a.shape[0] // steps, a.shape[1]), lambda b, j: (b * nj + j, 0))

    cast_specs = [row_block(a) for a in also_cast]
    outs = pl.pallas_call(
        functools.partial(_attn_kernel, nb=nb, lq=lq, n_cast=len(also_cast)),
        grid=(n_seq // nb, nj),
        in_specs=[seq_tile, mem_half(0), mem_half(1), mem_half(0), mem_half(1),
                  _layer_resident(g, layer), _layer_resident(w_q, layer),
                  _layer_resident(w_o, layer)] + cast_specs,
        out_specs=[seq_tile] + cast_specs,
        out_shape=[jax.ShapeDtypeStruct(x.shape, F32)]
                  + [jax.ShapeDtypeStruct(a.shape, BF16) for a in also_cast],
        scratch_shapes=[pltpu.VMEM((nb, N_MEM, D_MODEL), BF16),
                        pltpu.VMEM((nb, N_MEM, D_MODEL), BF16),
                        pltpu.VMEM((nb * lq, D_MODEL), F32)],
        compiler_params=pltpu.CompilerParams(
            dimension_semantics=("arbitrary", "arbitrary"),
            vmem_limit_bytes=VMEM_LIMIT_BYTES),
        name="cross_attn",
    )(x, mem_k, mem_k, mem_v, mem_v, g, w_q, w_o, *also_cast)
    return outs[0], tuple(outs[1:])


def _ffn_kernel(x_ref, g_ref, w1_ref, w2_ref, gf_ref, y_ref, *, final_norm):
    x = x_ref[...]
    h = _rmsnorm(x, g_ref[...]).astype(BF16)
    y = x
    for c in range(D_FF // D_MODEL):
        u = jnp.square(jnp.maximum(_dot(h, w1_ref[:, c * D_MODEL:(c + 1) * D_MODEL]), 0.0))
        y = y + _dot(u.astype(BF16), w2_ref[c * D_MODEL:(c + 1) * D_MODEL, :])
    if final_norm:
        y = _rmsnorm(y, gf_ref[...])
    y_ref[...] = y


def _ffn(x, layer, g, w1, w2, g_final, *, tm, final_norm):
    t = x.shape[0]
    assert t % tm == 0
    row_tile = pl.BlockSpec((tm, D_MODEL), lambda i: (i, 0))
    return pl.pallas_call(
        functools.partial(_ffn_kernel, final_norm=final_norm),
        grid=(t // tm,),
        in_specs=[row_tile, _layer_resident(g, layer), _layer_resident(w1, layer),
                  _layer_resident(w2, layer), _resident(g_final.shape)],
        out_specs=row_tile,
        out_shape=jax.ShapeDtypeStruct(x.shape, F32),
        compiler_params=pltpu.CompilerParams(
            dimension_semantics=("arbitrary",), vmem_limit_bytes=VMEM_LIMIT_BYTES),
        name="ffn",
    )(x, g, w1, w2, g_final)


def _ffn_and_attn_kernel(xf_ref, gf_ref, w1_ref, w2_ref, gfin_ref,
                         xa_ref, klo_ref, khi_ref, vlo_ref, vhi_ref, ga_ref, wq_ref, wo_ref,
                         yf_ref, ya_ref, k_buf, v_buf, o_buf, *, final_norm, nb, lq):
    _ffn_kernel(xf_ref, gf_ref, w1_ref, w2_ref, gfin_ref, yf_ref, final_norm=final_norm)
    _attn_body(xa_ref, klo_ref, khi_ref, vlo_ref, vhi_ref, ga_ref, wq_ref, wo_ref, ya_ref,
               k_buf, v_buf, o_buf, nb=nb, lq=lq, new_memory=None)


def _ffn_and_attn(x_ffn, x_attn, mem_k, mem_v, layer, lw, g_final, *, tm, final_norm):
    t = x_ffn.shape[0]
    n_seq, lq, _ = x_attn.shape
    assert t % tm == 0 and n_seq % (t // tm) == 0 and lq % SUBLANES == 0
    steps = t // tm
    nb = n_seq // steps
    row_tile = pl.BlockSpec((tm, D_MODEL), lambda i: (i, 0))
    seq_tile = pl.BlockSpec((nb, lq, D_MODEL), lambda i: (i, 0, 0))

    def mem_half(part):
        return pl.BlockSpec((None, nb, N_MEM, N_XHEADS, LANES), lambda i: (layer, i, 0, 0, part))

    return pl.pallas_call(
        functools.partial(_ffn_and_attn_kernel, final_norm=final_norm, nb=nb, lq=lq),
        grid=(steps,),
        in_specs=[row_tile, _layer_resident(lw['g_ffn'], layer), _layer_resident(lw['w_ff1'], layer),
                  _layer_resident(lw['w_ff2'], layer), _resident(g_final.shape),
                  seq_tile, mem_half(0), mem_half(1), mem_half(0), mem_half(1),
                  _layer_resident(lw['g_x'], layer), _layer_resident(lw['w_xq'], layer),
                  _layer_resident(lw['w_xo'], layer)],
        out_specs=[row_tile, seq_tile],
        out_shape=[jax.ShapeDtypeStruct(x_ffn.shape, F32), jax.ShapeDtypeStruct(x_attn.shape, F32)],
        scratch_shapes=[pltpu.VMEM((nb, N_MEM, D_MODEL), BF16),
                        pltpu.VMEM((nb, N_MEM, D_MODEL), BF16),
                        pltpu.VMEM((nb * lq, D_MODEL), F32)],
        compiler_params=pltpu.CompilerParams(
            dimension_semantics=("arbitrary",), vmem_limit_bytes=VMEM_LIMIT_BYTES),
        name="ffn_and_attn",
    )(x_ffn, lw['g_ffn'], lw['w_ff1'], lw['w_ff2'], g_final,
      x_attn, mem_k, mem_k, mem_v, mem_v, lw['g_x'], lw['w_xq'], lw['w_xo'])


def _mem_kv_kernel(m_ref, g_ref, *refs, nb):
    wk_refs, wv_refs = refs[:N_XHEADS], refs[N_XHEADS:2 * N_XHEADS]
    k_ref, v_ref = refs[2 * N_XHEADS:]
    m = _rmsnorm(m_ref[...], g_ref[...]).astype(BF16)
    for w_refs, o_ref in ((wk_refs, k_ref), (wv_refs, v_ref)):
        cols = jnp.concatenate([r[...].astype(BF16) for r in w_refs], axis=-1)
        y = _dot(m, cols)
        flat = o_ref.reshape(nb * N_MEM * N_XHEADS, LANES)
        for b in range(nb):
            for hd in range(N_XHEADS):
                rows = pl.ds(b * N_MEM * N_XHEADS + hd, N_MEM, stride=N_XHEADS)
                flat[rows, :] = y[b * N_MEM:(b + 1) * N_MEM, hd * LANES:(hd + 1) * LANES]


def _mem_kv(mem, g, w_k, w_v, *, nb):
    n_seq = mem.shape[0]
    assert n_seq % nb == 0 and XHEAD_DIM == 2 * LANES
    rows = nb * N_MEM
    shape = (DEPTH, n_seq, N_MEM, N_XHEADS, XHEAD_DIM)

    def head_half_cols(hd):
        return pl.BlockSpec((None, D_MODEL, LANES), lambda l, part, i: (l, 0, 2 * hd + part))

    w_specs = [head_half_cols(hd) for hd in range(N_XHEADS)]
    out_half = pl.BlockSpec((None, nb, N_MEM, N_XHEADS, LANES), lambda l, part, i: (l, i, 0, 0, part))
    return pl.pallas_call(
        functools.partial(_mem_kv_kernel, nb=nb),
        grid=(DEPTH, 2, n_seq // nb),
        in_specs=[pl.BlockSpec((rows, D_MODEL), lambda l, part, i: (i, 0)),
                  pl.BlockSpec((None, 1, D_MODEL), lambda l, part, i: (l, 0, 0))]
                 + w_specs + w_specs,
        out_specs=[out_half] * 2,
        out_shape=[jax.ShapeDtypeStruct(shape, F32)] * 2,
        compiler_params=pltpu.CompilerParams(
            dimension_semantics=("arbitrary", "arbitrary", "arbitrary"),
            vmem_limit_bytes=VMEM_LIMIT_BYTES),
        name="mem_kv",
    )(mem.reshape(n_seq * N_MEM, D_MODEL), g, *([w_k] * N_XHEADS), *([w_v] * N_XHEADS))


def kernel(x_prompt, x_sample, state_pool, state_sconv, state_cconv, cache_mem_k, cache_mem_v, mem_prompt, norm_mix_g, w_in, pool_w, pool_scale, sconv_w, sgu_ln_g, sgu_ln_b, sgu_w, sgu_b, cconv_w, cconv_b, cconv_ln_g, cconv_ln_b, w_branch, b_gate, w_mix_out, norm_x_g, norm_mem_g, w_xq, w_xk, w_xv, w_xo, norm_ffn_g, w_ff1, w_ff2, norm_f_g):
    batch, seq, _ = x_prompt.shape
    dec_batch, dec_seq, _ = x_sample.shape

    def rows(v):
        return v.reshape(DEPTH, 1, -1)

    lw_p = {
        'g_mix': rows(norm_mix_g), 'w_in': w_in.astype(BF16),
        'pool_w': pool_w.astype(BF16), 'pool_scale': rows(pool_scale),
        'sconv_w': sconv_w, 'sgu_ln_g': rows(sgu_ln_g), 'sgu_ln_b': rows(sgu_ln_b),
        'sgu_w': sgu_w, 'sgu_bcol': jnp.swapaxes(sgu_b, 1, 2),
        'cconv_w': cconv_w, 'cconv_b': rows(cconv_b),
        'cconv_ln_g': rows(cconv_ln_g), 'cconv_ln_b': rows(cconv_ln_b),
        'w_branch': w_branch.astype(BF16), 'b_gate': b_gate,
        'w_mix_out': w_mix_out.astype(BF16),
        'g_x': rows(norm_x_g), 'w_xq': w_xq.astype(BF16), 'w_xo': w_xo.astype(BF16),
        'g_ffn': rows(norm_ffn_g),
    }
    reps = CHUNK // dec_seq
    lw_s = dict(lw_p,
                sgu_w=jnp.tile(sgu_w[:, :, :dec_seq, :dec_seq], (1, 1, reps, reps)),
                sgu_bcol=jnp.swapaxes(jnp.tile(sgu_b[:, :, :dec_seq], (1, 1, reps)), 1, 2))
    g_final = norm_f_g.reshape(1, -1)

    mem_k_prompt, mem_v_prompt = _mem_kv(mem_prompt, rows(norm_mem_g), w_xk, w_xv, nb=4)
    zeros = lambda n_seq, n: jnp.zeros((DEPTH, n_seq, n, W_BR), F32)
    hist_p = (zeros(batch, POOL_PREV), zeros(batch, SCONV_K - 1), zeros(batch, CCONV_K - 1))
    hist_s = (state_pool, state_sconv, state_cconv)
    new_p = tuple(jnp.zeros_like(s) for s in hist_p)
    new_s = tuple(jnp.zeros_like(s) for s in hist_s) + (zeros(dec_batch, dec_seq),)
    x_p, x_s = x_prompt, x_sample
    for l in range(DEPTH):
        last = l == DEPTH - 1
        x_p, new_p = _mixer(x_p, *hist_p, l, new_p, lw_p, nb=1, lc=512, pos0=0, emit_v=False)
        to_cast = () if l else (w_ff1.reshape(DEPTH * D_MODEL, D_FF), w_ff2.reshape(DEPTH * D_FF, D_MODEL))
        x_p, cast = _attn(x_p, mem_k_prompt, mem_v_prompt, l, lw_p['g_x'], lw_p['w_xq'], lw_p['w_xo'],
                          nb=1, lq=1024, also_cast=to_cast)
        if cast:
            lw_ffn = dict(lw_p, w_ff1=cast[0].reshape(w_ff1.shape), w_ff2=cast[1].reshape(w_ff2.shape))
        x_s, new_s = _mixer(x_s, *hist_s, l, new_s, lw_s, nb=256 // dec_seq, lc=dec_seq,
                            pos0=PAST_LEN, emit_v=True)
        x_p, x_s = _ffn_and_attn(x_p.reshape(batch * seq, D_MODEL), x_s, cache_mem_k, cache_mem_v,
                                 l, lw_ffn, g_final, tm=512, final_norm=last)
        x_p = x_p.reshape(batch, seq, D_MODEL)
        x_s = _ffn(x_s.reshape(dec_batch * dec_seq, D_MODEL), l, lw_ffn['g_ffn'], lw_ffn['w_ff1'],
                   lw_ffn['w_ff2'], g_final, tm=512, final_norm=last).reshape(x_sample.shape)

    return (x_p, x_s) + new_p + (mem_k_prompt, mem_v_prompt) + new_s
```

```python
import functools

import jax
import jax.numpy as jnp
from jax import lax
from jax.experimental import pallas as pl
from jax.experimental.pallas import tpu as pltpu

D_MODEL = 1024
DEPTH = 2
PAST_LEN = 16384
W_BR = D_MODEL // 2
POOL_WINDOWS = (2, 4, 8, 16)
POOL_GW = W_BR // len(POOL_WINDOWS)
POOL_PREV = max(POOL_WINDOWS) - 1
SCONV_K = 3
CCONV_K = 31
CHUNK = 128
N_SGU_GROUPS = 4
SGU_GW = W_BR // N_SGU_GROUPS
N_MEM = 256
N_XHEADS = 4
XHEAD_DIM = D_MODEL // N_XHEADS
D_FF = 4 * D_MODEL
N_BRANCH = 4
EPS = 1e-6
GATE_COL0 = 8 * W_BR

SUBLANES = 8
LANES = 128
N_PLANES = W_BR // LANES
POOL_OFF = 16
SCONV_OFF = 8
CCONV_OFF = 32

VMEM_LIMIT_BYTES = 56 * 1024 * 1024

BF16 = jnp.bfloat16
F32 = jnp.float32


def _dot(a, b):
    return jnp.dot(a, b, preferred_element_type=F32)


def _rmsnorm(x, g):
    return x * lax.rsqrt(jnp.mean(x * x, axis=-1, keepdims=True) + EPS) * g


def _layernorm(x, g, b):
    xc = x - jnp.mean(x, axis=-1, keepdims=True)
    var = jnp.mean(xc * xc, axis=-1, keepdims=True)
    return xc * lax.rsqrt(var + EPS) * g + b


def _gelu_tanh(x):
    return x * (0.5 * (1.0 + jnp.tanh(0.7978845608028654 * (x + 0.044715 * (x * x * x)))))


def _resident(shape):
    nd = len(shape)
    return pl.BlockSpec(shape, lambda *_: (0,) * nd, pipeline_mode=pl.Buffered(1))


def _layer_resident(stacked, layer):
    index = (layer if stacked.shape[0] > 1 else 0,) + (0,) * (stacked.ndim - 1)
    return pl.BlockSpec((None,) + stacked.shape[1:], lambda *_: index,
                        pipeline_mode=pl.Buffered(1))


def _store_planes(ext, row0, value):
    rows = value.shape[1]
    for c in range(N_PLANES):
        ext[c, :, row0:row0 + rows, :] = value[:, :, c * LANES:(c + 1) * LANES]


def _load_planes(ext, row0, rows):
    return jnp.concatenate([ext[c, :, row0:row0 + rows, :] for c in range(N_PLANES)], axis=-1)


def _causal_dwconv(ext, w_ref, row0, taps, rows):
    planes = []
    for c in range(N_PLANES):
        cs = slice(c * LANES, (c + 1) * LANES)
        acc = w_ref[0:1, cs][None] * ext[c, :, row0:row0 + rows, :]
        for k in range(1, taps):
            acc = acc + w_ref[k:k + 1, cs][None] * ext[c, :, row0 + k:row0 + k + rows, :]
        planes.append(acc)
    return jnp.concatenate(planes, axis=-1)


def _mixer_kernel(x_ref, pool_st_ref, sconv_st_ref, cconv_st_ref,
                  g_mix_ref, w_in_ref, pool_w_ref, pool_scale_ref, sconv_w_ref,
                  sgu_g_ref, sgu_b_ref, sgu_w_ref, sgu_bcol_ref,
                  cconv_w_ref, cconv_b_ref, cln_g_ref, cln_b_ref,
                  w_br_ref, b_gate_ref, w_mix_ref, *rest,
                  nb, lc, nj, pos0, emit_v, n_aliased):
    y_ref, pool_out_ref, sconv_out_ref, cconv_out_ref, *rest = rest[n_aliased:]
    if emit_v:
        v_out_ref, pool_ext, sconv_ext, cconv_ext = rest
    else:
        pool_ext, sconv_ext, cconv_ext = rest
    j = pl.program_id(1)
    tm = nb * lc
    pool_h0 = POOL_OFF - POOL_PREV
    sconv_h0 = SCONV_OFF - (SCONV_K - 1)
    cconv_h0 = CCONV_OFF - (CCONV_K - 1)

    x = x_ref[...].reshape(tm, D_MODEL)
    h = _rmsnorm(x, g_mix_ref[...]).astype(BF16)

    @pl.when(j == 0)
    def _load_history():
        _store_planes(pool_ext, pool_h0, pool_st_ref[...])
        _store_planes(sconv_ext, sconv_h0, sconv_st_ref[...])
        _store_planes(cconv_ext, cconv_h0, cconv_st_ref[...])

    def carry(ext, h0, prev, out_ref):
        new = _load_planes(ext, lc + h0, prev)
        out_ref[...] = new
        if nj > 1:
            _store_planes(ext, h0, new)

    d_in = _dot(h, w_in_ref[:, 6 * W_BR:8 * W_BR])
    c_uv = _dot(h, w_in_ref[:, 4 * W_BR:6 * W_BR])
    z_b = _dot(h, w_in_ref[:, W_BR:4 * W_BR])
    a_u = _dot(h, w_in_ref[:, 0:W_BR])
    gates = [jax.nn.sigmoid(_dot(h, w_in_ref[:, GATE_COL0 + k * D_MODEL:GATE_COL0 + (k + 1) * D_MODEL])
                            + b_gate_ref[k:k + 1, :]) for k in range(N_BRANCH)]

    glu = d_in[:, :W_BR] * jax.nn.sigmoid(d_in[:, W_BR:])
    _store_planes(cconv_ext, CCONV_OFF, glu.reshape(nb, lc, W_BR))
    d_c = _causal_dwconv(cconv_ext, cconv_w_ref, cconv_h0, CCONV_K, lc).reshape(tm, W_BR)
    d_ln = _layernorm(d_c + cconv_b_ref[...], cln_g_ref[...], cln_b_ref[...])
    d_out = d_ln * jax.nn.sigmoid(d_ln)
    carry(cconv_ext, cconv_h0, CCONV_K - 1, cconv_out_ref)

    c_uv = _gelu_tanh(c_uv)
    c_u, c_v = c_uv[:, :W_BR], c_uv[:, W_BR:]
    c_vn = _layernorm(c_v, sgu_g_ref[...], sgu_b_ref[...])
    if emit_v:
        v_out_ref[...] = c_vn.reshape(nb, lc, W_BR)
    row = lax.broadcasted_iota(jnp.int32, (CHUNK, CHUNK), 0)
    col = lax.broadcasted_iota(jnp.int32, (CHUNK, CHUNK), 1)
    keep = row >= col
    seg = min(lc, CHUNK)
    if seg < CHUNK:
        keep = keep & ((row // seg) == (col // seg))
    c_vn_bf = c_vn.astype(BF16)
    s_cols = []
    for g in range(N_SGU_GROUPS):
        sl = slice(g * SGU_GW, (g + 1) * SGU_GW)
        w_g = jnp.where(keep, sgu_w_ref[g], 0.0).astype(BF16)
        bias = sgu_bcol_ref[:, g:g + 1]
        s_rows = [_dot(w_g, c_vn_bf[c * CHUNK:(c + 1) * CHUNK, sl]) + bias
                  for c in range(tm // CHUNK)]
        s_cols.append(jnp.concatenate(s_rows, axis=0))
    c_out = c_u * jnp.concatenate(s_cols, axis=-1)

    b_h, b_b, b_c = z_b[:, :W_BR], z_b[:, W_BR:2 * W_BR], z_b[:, 2 * W_BR:]
    _store_planes(sconv_ext, SCONV_OFF, (b_c * b_h).reshape(nb, lc, W_BR))
    conv = _causal_dwconv(sconv_ext, sconv_w_ref, sconv_h0, SCONV_K, lc)
    b_out = b_b * conv.reshape(tm, W_BR)
    carry(sconv_ext, sconv_h0, SCONV_K - 1, sconv_out_ref)

    _store_planes(pool_ext, POOL_OFF, a_u.reshape(nb, lc, W_BR))
    pos = pos0 + j * lc + lax.broadcasted_iota(jnp.int32, (1, lc, POOL_GW), 1)
    a_parts = []
    for g, w in enumerate(POOL_WINDOWS):
        cur = pool_ext[g, :, POOL_OFF:POOL_OFF + lc, :]
        win = cur
        for i in range(1, w):
            win = win + pool_ext[g, :, POOL_OFF - i:POOL_OFF - i + lc, :]
        cnt = jnp.minimum(pos + 1, w).astype(F32)
        p = win / cnt - cur
        a_parts.append(_dot(p.reshape(tm, POOL_GW).astype(BF16), pool_w_ref[g]))
    a_out = jnp.concatenate(a_parts, axis=-1) * pool_scale_ref[...]
    carry(pool_ext, pool_h0, POOL_PREV, pool_out_ref)

    merged = None
    for k, branch in enumerate((a_out, b_out, c_out, d_out)):
        term = gates[k] * _dot(branch.astype(BF16), w_br_ref[k])
        merged = term if merged is None else merged + term

    y = x + _dot(merged.astype(BF16), w_mix_ref[...])
    y_ref[...] = y.reshape(nb, lc, D_MODEL)


def _mixer(x, pool_st, sconv_st, cconv_st, layer, stacked, lw, *, nb, lc, pos0, emit_v):
    n_seq, seq_len, _ = x.shape
    assert n_seq % nb == 0 and seq_len % lc == 0 and (nb * lc) % CHUNK == 0
    assert lc % SUBLANES == 0 and (lc % CHUNK == 0 or CHUNK % lc == 0)
    assert lc == seq_len or lc >= CCONV_K - 1
    nj = seq_len // lc
    grid = (n_seq // nb, nj)

    def state_tile(rows):
        return pl.BlockSpec((None, nb, rows, W_BR), lambda b, j: (layer, b, 0, 0))

    seq_tile = pl.BlockSpec((nb, lc, D_MODEL), lambda b, j: (b, j, 0))
    params = (lw['g_mix'], lw['w_in'], lw['pool_w'], lw['pool_scale'], lw['sconv_w'],
              lw['sgu_ln_g'], lw['sgu_ln_b'], lw['sgu_w'], lw['sgu_bcol'],
              lw['cconv_w'], lw['cconv_b'], lw['cconv_ln_g'], lw['cconv_ln_b'],
              lw['w_branch'], lw['b_gate'], lw['w_mix_out'])
    in_specs = [seq_tile, state_tile(POOL_PREV), state_tile(SCONV_K - 1), state_tile(CCONV_K - 1)]
    in_specs += [_layer_resident(p, layer) for p in params]
    in_specs += [pl.BlockSpec(memory_space=pl.ANY)] * len(stacked)
    out_shape = [jax.ShapeDtypeStruct(x.shape, F32)]
    out_shape += [jax.ShapeDtypeStruct(s.shape, s.dtype) for s in stacked]
    out_specs = [seq_tile, state_tile(POOL_PREV), state_tile(SCONV_K - 1), state_tile(CCONV_K - 1)]
    if emit_v:
        out_specs.append(pl.BlockSpec((None, nb, lc, W_BR), lambda b, j: (layer, b, j, 0)))
    assert len(stacked) == len(out_specs) - 1
    first_stacked_in = 4 + len(params)
    outs = pl.pallas_call(
        functools.partial(_mixer_kernel, nb=nb, lc=lc, nj=nj, pos0=pos0, emit_v=emit_v,
                          n_aliased=len(stacked)),
        grid=grid, in_specs=in_specs, out_specs=out_specs, out_shape=out_shape,
        input_output_aliases={first_stacked_in + i: 1 + i for i in range(len(stacked))},
        scratch_shapes=[pltpu.VMEM((N_PLANES, nb, POOL_OFF + lc, LANES), F32),
                        pltpu.VMEM((N_PLANES, nb, SCONV_OFF + lc, LANES), F32),
                        pltpu.VMEM((N_PLANES, nb, CCONV_OFF + lc, LANES), F32)],
        compiler_params=pltpu.CompilerParams(
            dimension_semantics=("arbitrary", "arbitrary"),
            vmem_limit_bytes=VMEM_LIMIT_BYTES),
        name="mixer",
    )(x, pool_st, sconv_st, cconv_st, *params, *stacked)
    return outs[0], tuple(outs[1:])


def _attn_body(x_ref, klo_ref, khi_ref, vlo_ref, vhi_ref, g_ref, wq_ref, wo_ref, y_ref,
               k_buf, v_buf, o_buf, *, nb, lq, new_memory):
    half = XHEAD_DIM // 2

    def _gather_heads():
        for lo_ref, hi_ref, buf in ((klo_ref, khi_ref, k_buf), (vlo_ref, vhi_ref, v_buf)):
            lo = lo_ref.reshape(nb * N_MEM * N_XHEADS, half)
            hi = hi_ref.reshape(nb * N_MEM * N_XHEADS, half)
            for b in range(nb):
                for hd in range(N_XHEADS):
                    rows = pl.ds(b * N_MEM * N_XHEADS + hd, N_MEM, stride=N_XHEADS)
                    c0 = hd * XHEAD_DIM
                    buf[b, :, c0:c0 + half] = lo[rows, :].astype(BF16)
                    buf[b, :, c0 + half:c0 + XHEAD_DIM] = hi[rows, :].astype(BF16)

    if new_memory is None:
        _gather_heads()
    else:
        pl.when(new_memory)(_gather_heads)

    x = x_ref[...].reshape(nb * lq, D_MODEL)
    h = _rmsnorm(x, g_ref[...]).astype(BF16)
    q = _dot(h, wq_ref[...])
    pairs = [(b, hd) for b in range(nb) for hd in range(N_XHEADS)]
    cols = lambda hd: slice(hd * XHEAD_DIM, (hd + 1) * XHEAD_DIM)
    s = jnp.concatenate(
        [lax.dot_general(q[b * lq:(b + 1) * lq, cols(hd)].astype(BF16), k_buf[b, :, cols(hd)],
                         (((1,), (1,)), ((), ())), preferred_element_type=F32)
         for b, hd in pairs], axis=0) * (XHEAD_DIM ** -0.5)
    e = jnp.exp(s - jnp.max(s, axis=-1, keepdims=True))
    p = e / jnp.sum(e, axis=-1, keepdims=True)
    for i, (b, hd) in enumerate(pairs):
        o_buf[b * lq:(b + 1) * lq, cols(hd)] = _dot(p[i * lq:(i + 1) * lq].astype(BF16),
                                                    v_buf[b, :, cols(hd)])
    y = x + _dot(o_buf[...].astype(BF16), wo_ref[...])
    y_ref[...] = y.reshape(nb, lq, D_MODEL)


N_ATTN_INPUTS = 8


def _attn_kernel(*refs, nb, lq, n_cast):
    ins, refs = refs[:N_ATTN_INPUTS], refs[N_ATTN_INPUTS:]
    cast_in, refs = refs[:n_cast], refs[n_cast:]
    y_ref, refs = refs[0], refs[1:]
    cast_out, scratch = refs[:n_cast], refs[n_cast:]
    _attn_body(*ins, y_ref, *scratch, nb=nb, lq=lq, new_memory=pl.program_id(1) == 0)
    for src_ref, dst_ref in zip(cast_in, cast_out):
        dst_ref[...] = src_ref[...].astype(BF16)


def _attn(x, mem_k, mem_v, layer, g, w_q, w_o, *, nb, lq, also_cast=()):
    n_seq, seq_len, _ = x.shape
    assert n_seq % nb == 0 and seq_len % lq == 0 and lq % SUBLANES == 0
    assert XHEAD_DIM == 2 * LANES
    nj = seq_len // lq
    steps = (n_seq // nb) * nj
    seq_tile = pl.BlockSpec((nb, lq, D_MODEL), lambda b, j: (b, j, 0))

    def mem_half(part):
        return pl.BlockSpec((None, nb, N_MEM, N_XHEADS, LANES),
                            lambda b, j: (layer, b, 0, 0, part))

    def row_block(a, row0, n_rows):
        assert a.ndim == 2 and n_rows % (steps * 2 * SUBLANES) == 0
        rows = n_rows // steps
        assert row0 % rows == 0
        return pl.BlockSpec((rows, a.shape[1]), lambda b, j: (row0 // rows + b * nj + j, 0))

    cast_in_specs = [row_block(a, row0, n_rows) for a, row0, n_rows in also_cast]
    cast_out_specs = [row_block(a, 0, n_rows) for a, _, n_rows in also_cast]
    outs = pl.pallas_call(
        functools.partial(_attn_kernel, nb=nb, lq=lq, n_cast=len(also_cast)),
        grid=(n_seq // nb, nj),
        in_specs=[seq_tile, mem_half(0), mem_half(1), mem_half(0), mem_half(1),
                  _layer_resident(g, layer), _layer_resident(w_q, layer),
                  _layer_resident(w_o, layer)] + cast_in_specs,
        out_specs=[seq_tile] + cast_out_specs,
        out_shape=[jax.ShapeDtypeStruct(x.shape, F32)]
                  + [jax.ShapeDtypeStruct((n_rows, a.shape[1]), BF16) for a, _, n_rows in also_cast],
        scratch_shapes=[pltpu.VMEM((nb, N_MEM, D_MODEL), BF16),
                        pltpu.VMEM((nb, N_MEM, D_MODEL), BF16),
                        pltpu.VMEM((nb * lq, D_MODEL), F32)],
        compiler_params=pltpu.CompilerParams(
            dimension_semantics=("arbitrary", "arbitrary"),
            vmem_limit_bytes=VMEM_LIMIT_BYTES),
        name="cross_attn",
    )(x, mem_k, mem_k, mem_v, mem_v, g, w_q, w_o, *(a for a, _, _ in also_cast))
    return outs[0], tuple(outs[1:])


def _ffn_kernel(x_ref, g_ref, w1_ref, w2_ref, gf_ref, y_ref, *, final_norm):
    x = x_ref[...]
    h = _rmsnorm(x, g_ref[...]).astype(BF16)
    y = x
    for c in range(D_FF // D_MODEL):
        u = jnp.square(jnp.maximum(_dot(h, w1_ref[:, c * D_MODEL:(c + 1) * D_MODEL]), 0.0))
        y = y + _dot(u.astype(BF16), w2_ref[c * D_MODEL:(c + 1) * D_MODEL, :])
    if final_norm:
        y = _rmsnorm(y, gf_ref[...])
    y_ref[...] = y


def _ffn(x, layer, g, w1, w2, g_final, *, tm, final_norm):
    t = x.shape[0]
    assert t % tm == 0
    row_tile = pl.BlockSpec((tm, D_MODEL), lambda i: (i, 0))
    return pl.pallas_call(
        functools.partial(_ffn_kernel, final_norm=final_norm),
        grid=(t // tm,),
        in_specs=[row_tile, _layer_resident(g, layer), _layer_resident(w1, layer),
                  _layer_resident(w2, layer), _resident(g_final.shape)],
        out_specs=row_tile,
        out_shape=jax.ShapeDtypeStruct(x.shape, F32),
        compiler_params=pltpu.CompilerParams(
            dimension_semantics=("arbitrary",), vmem_limit_bytes=VMEM_LIMIT_BYTES),
        name="ffn",
    )(x, g, w1, w2, g_final)


def _ffn_and_attn_kernel(xf_ref, gf_ref, w1_ref, w2_ref, gfin_ref,
                         xa_ref, klo_ref, khi_ref, vlo_ref, vhi_ref, ga_ref, wq_ref, wo_ref,
                         yf_ref, ya_ref, k_buf, v_buf, o_buf, *, final_norm, nb, lq):
    _ffn_kernel(xf_ref, gf_ref, w1_ref, w2_ref, gfin_ref, yf_ref, final_norm=final_norm)
    _attn_body(xa_ref, klo_ref, khi_ref, vlo_ref, vhi_ref, ga_ref, wq_ref, wo_ref, ya_ref,
               k_buf, v_buf, o_buf, nb=nb, lq=lq, new_memory=None)


def _ffn_and_attn(x_ffn, x_attn, mem_k, mem_v, layer, lw, g_final, *, tm, final_norm):
    t = x_ffn.shape[0]
    n_seq, lq, _ = x_attn.shape
    assert t % tm == 0 and n_seq % (t // tm) == 0 and lq % SUBLANES == 0
    steps = t // tm
    nb = n_seq // steps
    row_tile = pl.BlockSpec((tm, D_MODEL), lambda i: (i, 0))
    seq_tile = pl.BlockSpec((nb, lq, D_MODEL), lambda i: (i, 0, 0))

    def mem_half(part):
        return pl.BlockSpec((None, nb, N_MEM, N_XHEADS, LANES), lambda i: (layer, i, 0, 0, part))

    return pl.pallas_call(
        functools.partial(_ffn_and_attn_kernel, final_norm=final_norm, nb=nb, lq=lq),
        grid=(steps,),
        in_specs=[row_tile, _layer_resident(lw['g_ffn'], layer), _layer_resident(lw['w_ff1'], layer),
                  _layer_resident(lw['w_ff2'], layer), _resident(g_final.shape),
                  seq_tile, mem_half(0), mem_half(1), mem_half(0), mem_half(1),
                  _layer_resident(lw['g_x'], layer), _layer_resident(lw['w_xq'], layer),
                  _layer_resident(lw['w_xo'], layer)],
        out_specs=[row_tile, seq_tile],
        out_shape=[jax.ShapeDtypeStruct(x_ffn.shape, F32), jax.ShapeDtypeStruct(x_attn.shape, F32)],
        scratch_shapes=[pltpu.VMEM((nb, N_MEM, D_MODEL), BF16),
                        pltpu.VMEM((nb, N_MEM, D_MODEL), BF16),
                        pltpu.VMEM((nb * lq, D_MODEL), F32)],
        compiler_params=pltpu.CompilerParams(
            dimension_semantics=("arbitrary",), vmem_limit_bytes=VMEM_LIMIT_BYTES),
        name="ffn_and_attn",
    )(x_ffn, lw['g_ffn'], lw['w_ff1'], lw['w_ff2'], g_final,
      x_attn, mem_k, mem_k, mem_v, mem_v, lw['g_x'], lw['w_xq'], lw['w_xo'])


def _mem_kv_kernel(m_ref, g_ref, *refs, nb):
    wk_refs, wv_refs = refs[:N_XHEADS], refs[N_XHEADS:2 * N_XHEADS]
    k_ref, v_ref = refs[2 * N_XHEADS:]
    m = _rmsnorm(m_ref[...], g_ref[...]).astype(BF16)
    for w_refs, o_ref in ((wk_refs, k_ref), (wv_refs, v_ref)):
        cols = jnp.concatenate([r[...].astype(BF16) for r in w_refs], axis=-1)
        y = _dot(m, cols)
        flat = o_ref.reshape(nb * N_MEM * N_XHEADS, LANES)
        for b in range(nb):
            for hd in range(N_XHEADS):
                rows = pl.ds(b * N_MEM * N_XHEADS + hd, N_MEM, stride=N_XHEADS)
                flat[rows, :] = y[b * N_MEM:(b + 1) * N_MEM, hd * LANES:(hd + 1) * LANES]


def _mem_kv(mem, g, w_k, w_v, *, nb):
    n_seq = mem.shape[0]
    assert n_seq % nb == 0 and XHEAD_DIM == 2 * LANES
    rows = nb * N_MEM
    shape = (DEPTH, n_seq, N_MEM, N_XHEADS, XHEAD_DIM)

    def head_half_cols(hd):
        return pl.BlockSpec((None, D_MODEL, LANES), lambda l, part, i: (l, 0, 2 * hd + part))

    w_specs = [head_half_cols(hd) for hd in range(N_XHEADS)]
    out_half = pl.BlockSpec((None, nb, N_MEM, N_XHEADS, LANES), lambda l, part, i: (l, i, 0, 0, part))
    return pl.pallas_call(
        functools.partial(_mem_kv_kernel, nb=nb),
        grid=(DEPTH, 2, n_seq // nb),
        in_specs=[pl.BlockSpec((rows, D_MODEL), lambda l, part, i: (i, 0)),
                  pl.BlockSpec((None, 1, D_MODEL), lambda l, part, i: (l, 0, 0))]
                 + w_specs + w_specs,
        out_specs=[out_half] * 2,
        out_shape=[jax.ShapeDtypeStruct(shape, F32)] * 2,
        compiler_params=pltpu.CompilerParams(
            dimension_semantics=("arbitrary", "arbitrary", "arbitrary"),
            vmem_limit_bytes=VMEM_LIMIT_BYTES),
        name="mem_kv",
    )(mem.reshape(n_seq * N_MEM, D_MODEL), g, *([w_k] * N_XHEADS), *([w_v] * N_XHEADS))


def kernel(x_prompt, x_sample, state_pool, state_sconv, state_cconv, cache_mem_k, cache_mem_v, mem_prompt, norm_mix_g, w_in, pool_w, pool_scale, sconv_w, sgu_ln_g, sgu_ln_b, sgu_w, sgu_b, cconv_w, cconv_b, cconv_ln_g, cconv_ln_b, w_branch, b_gate, w_mix_out, norm_x_g, norm_mem_g, w_xq, w_xk, w_xv, w_xo, norm_ffn_g, w_ff1, w_ff2, norm_f_g):
    batch, seq, _ = x_prompt.shape
    dec_batch, dec_seq, _ = x_sample.shape

    def rows(v):
        return v.reshape(DEPTH, 1, -1)

    reps = CHUNK // dec_seq
    small = {
        'g_mix': rows(norm_mix_g), 'pool_w': pool_w.astype(BF16), 'pool_scale': rows(pool_scale),
        'sconv_w': sconv_w, 'sgu_ln_g': rows(sgu_ln_g), 'sgu_ln_b': rows(sgu_ln_b),
        'cconv_w': cconv_w, 'cconv_b': rows(cconv_b),
        'cconv_ln_g': rows(cconv_ln_g), 'cconv_ln_b': rows(cconv_ln_b),
        'b_gate': b_gate, 'g_x': rows(norm_x_g), 'g_ffn': rows(norm_ffn_g),
    }
    sgu_p = {'sgu_w': sgu_w, 'sgu_bcol': jnp.swapaxes(sgu_b, 1, 2)}
    sgu_s = {'sgu_w': jnp.tile(sgu_w[:, :, :dec_seq, :dec_seq], (1, 1, reps, reps)),
             'sgu_bcol': jnp.swapaxes(jnp.tile(sgu_b[:, :, :dec_seq], (1, 1, reps)), 1, 2)}
    big = {'w_in': w_in, 'w_branch': w_branch, 'w_mix_out': w_mix_out, 'w_xq': w_xq, 'w_xo': w_xo}
    big0 = {k: w[:1].astype(BF16) for k, w in big.items()}
    as_rows = lambda w: w.reshape(-1, w.shape[-1])
    later = [(as_rows(w_ff1), 0, DEPTH * D_MODEL), (as_rows(w_ff2), 0, DEPTH * D_FF)]
    later += [(as_rows(w), as_rows(w).shape[0] // DEPTH, as_rows(w).shape[0] // DEPTH)
              for w in big.values()]
    g_final = norm_f_g.reshape(1, -1)

    mem_k_prompt, mem_v_prompt = _mem_kv(mem_prompt, rows(norm_mem_g), w_xk, w_xv, nb=4)
    zeros = lambda n_seq, n: jnp.zeros((DEPTH, n_seq, n, W_BR), F32)
    hist_p = (zeros(batch, POOL_PREV), zeros(batch, SCONV_K - 1), zeros(batch, CCONV_K - 1))
    hist_s = (state_pool, state_sconv, state_cconv)
    new_p = tuple(jnp.zeros_like(s) for s in hist_p)
    new_s = tuple(jnp.zeros_like(s) for s in hist_s) + (zeros(dec_batch, dec_seq),)
    x_p, x_s = x_prompt, x_sample
    weights = big0
    for l in range(DEPTH):
        last = l == DEPTH - 1
        lw_p, lw_s = dict(small, **sgu_p, **weights), dict(small, **sgu_s, **weights)
        x_p, new_p = _mixer(x_p, *hist_p, l, new_p, lw_p, nb=1, lc=512, pos0=0, emit_v=False)
        x_p, cast = _attn(x_p, mem_k_prompt, mem_v_prompt, l, lw_p['g_x'], lw_p['w_xq'], lw_p['w_xo'],
                          nb=1, lq=1024, also_cast=() if l else later)
        if cast:
            ffn_w = {'w_ff1': cast[0].reshape(w_ff1.shape), 'w_ff2': cast[1].reshape(w_ff2.shape)}
            next_weights = {k: c.reshape((1,) + w.shape[1:]) for (k, w), c in zip(big.items(), cast[2:])}
        x_s, new_s = _mixer(x_s, *hist_s, l, new_s, lw_s, nb=256 // dec_seq, lc=dec_seq,
                            pos0=PAST_LEN, emit_v=True)
        lw_ffn = dict(lw_p, **ffn_w)
        x_p, x_s = _ffn_and_attn(x_p.reshape(batch * seq, D_MODEL), x_s, cache_mem_k, cache_mem_v,
                                 l, lw_ffn, g_final, tm=512, final_norm=last)
        x_p = x_p.reshape(batch, seq, D_MODEL)
        x_s = _ffn(x_s.reshape(dec_batch * dec_seq, D_MODEL), l, lw_ffn['g_ffn'], lw_ffn['w_ff1'],
                   lw_ffn['w_ff2'], g_final, tm=512, final_norm=last).reshape(x_sample.shape)
        weights = next_weights

    return (x_p, x_s) + new_p + (mem_k_prompt, mem_v_prompt) + new_s
```

```python
import functools

import jax
import jax.numpy as jnp
from jax import lax
from jax.experimental import pallas as pl
from jax.experimental.pallas import tpu as pltpu

D_MODEL = 1024
DEPTH = 2
PAST_LEN = 16384
W_BR = D_MODEL // 2
POOL_WINDOWS = (2, 4, 8, 16)
POOL_GW = W_BR // len(POOL_WINDOWS)
POOL_PREV = max(POOL_WINDOWS) - 1
SCONV_K = 3
CCONV_K = 31
CHUNK = 128
N_SGU_GROUPS = 4
SGU_GW = W_BR // N_SGU_GROUPS
N_MEM = 256
N_XHEADS = 4
XHEAD_DIM = D_MODEL // N_XHEADS
D_FF = 4 * D_MODEL
N_BRANCH = 4
EPS = 1e-6
GATE_COL0 = 8 * W_BR

SUBLANES = 8
LANES = 128
N_PLANES = W_BR // LANES
POOL_OFF = 16
SCONV_OFF = 8
CCONV_OFF = 32

VMEM_LIMIT_BYTES = 56 * 1024 * 1024

BF16 = jnp.bfloat16
F32 = jnp.float32


def _dot(a, b):
    return jnp.dot(a, b, preferred_element_type=F32)


def _rmsnorm(x, g):
    return x * lax.rsqrt(jnp.mean(x * x, axis=-1, keepdims=True) + EPS) * g


def _layernorm(x, g, b):
    xc = x - jnp.mean(x, axis=-1, keepdims=True)
    var = jnp.mean(xc * xc, axis=-1, keepdims=True)
    return xc * lax.rsqrt(var + EPS) * g + b


def _gelu_tanh(x):
    return x * (0.5 * (1.0 + jnp.tanh(0.7978845608028654 * (x + 0.044715 * (x * x * x)))))


def _resident(shape):
    nd = len(shape)
    return pl.BlockSpec(shape, lambda *_: (0,) * nd, pipeline_mode=pl.Buffered(1))


def _layer_resident(stacked, layer):
    index = (layer if stacked.shape[0] > 1 else 0,) + (0,) * (stacked.ndim - 1)
    return pl.BlockSpec((None,) + stacked.shape[1:], lambda *_: index,
                        pipeline_mode=pl.Buffered(1))


def _side_casts(jobs, steps, step_of):
    def row_block(a, row0, n_rows):
        assert a.ndim == 2 and n_rows % (steps * 2 * SUBLANES) == 0
        rows = n_rows // steps
        assert row0 % rows == 0
        return pl.BlockSpec((rows, a.shape[1]), lambda *ids: (row0 // rows + step_of(*ids), 0))

    return ([row_block(a, row0, n_rows) for a, row0, n_rows in jobs],
            [row_block(a, 0, n_rows) for a, _, n_rows in jobs],
            [jax.ShapeDtypeStruct((n_rows, a.shape[1]), BF16) for a, _, n_rows in jobs])


def _cast_blocks(src_refs, dst_refs):
    for src_ref, dst_ref in zip(src_refs, dst_refs):
        dst_ref[...] = src_ref[...].astype(BF16)


def _store_planes(ext, row0, value):
    rows = value.shape[1]
    for c in range(N_PLANES):
        ext[c, :, row0:row0 + rows, :] = value[:, :, c * LANES:(c + 1) * LANES]


def _load_planes(ext, row0, rows):
    return jnp.concatenate([ext[c, :, row0:row0 + rows, :] for c in range(N_PLANES)], axis=-1)


def _causal_dwconv(ext, w_ref, row0, taps, rows):
    planes = []
    for c in range(N_PLANES):
        cs = slice(c * LANES, (c + 1) * LANES)
        acc = w_ref[0:1, cs][None] * ext[c, :, row0:row0 + rows, :]
        for k in range(1, taps):
            acc = acc + w_ref[k:k + 1, cs][None] * ext[c, :, row0 + k:row0 + k + rows, :]
        planes.append(acc)
    return jnp.concatenate(planes, axis=-1)


def _mixer_kernel(x_ref, pool_st_ref, sconv_st_ref, cconv_st_ref,
                  g_mix_ref, w_in_ref, pool_w_ref, pool_scale_ref, sconv_w_ref,
                  sgu_g_ref, sgu_b_ref, sgu_w_ref, sgu_bcol_ref,
                  cconv_w_ref, cconv_b_ref, cln_g_ref, cln_b_ref,
                  w_br_ref, b_gate_ref, w_mix_ref, *rest,
                  nb, lc, nj, pos0, emit_v, n_aliased):
    y_ref, pool_out_ref, sconv_out_ref, cconv_out_ref, *rest = rest[n_aliased:]
    if emit_v:
        v_out_ref, pool_ext, sconv_ext, cconv_ext = rest
    else:
        pool_ext, sconv_ext, cconv_ext = rest
    j = pl.program_id(1)
    tm = nb * lc
    pool_h0 = POOL_OFF - POOL_PREV
    sconv_h0 = SCONV_OFF - (SCONV_K - 1)
    cconv_h0 = CCONV_OFF - (CCONV_K - 1)

    x = x_ref[...].reshape(tm, D_MODEL)
    h = _rmsnorm(x, g_mix_ref[...]).astype(BF16)

    @pl.when(j == 0)
    def _load_history():
        _store_planes(pool_ext, pool_h0, pool_st_ref[...])
        _store_planes(sconv_ext, sconv_h0, sconv_st_ref[...])
        _store_planes(cconv_ext, cconv_h0, cconv_st_ref[...])

    def carry(ext, h0, prev, out_ref):
        new = _load_planes(ext, lc + h0, prev)
        out_ref[...] = new
        if nj > 1:
            _store_planes(ext, h0, new)

    d_in = _dot(h, w_in_ref[:, 6 * W_BR:8 * W_BR])
    c_uv = _dot(h, w_in_ref[:, 4 * W_BR:6 * W_BR])
    z_b = _dot(h, w_in_ref[:, W_BR:4 * W_BR])
    a_u = _dot(h, w_in_ref[:, 0:W_BR])
    gates = [jax.nn.sigmoid(_dot(h, w_in_ref[:, GATE_COL0 + k * D_MODEL:GATE_COL0 + (k + 1) * D_MODEL])
                            + b_gate_ref[k:k + 1, :]) for k in range(N_BRANCH)]

    glu = d_in[:, :W_BR] * jax.nn.sigmoid(d_in[:, W_BR:])
    _store_planes(cconv_ext, CCONV_OFF, glu.reshape(nb, lc, W_BR))
    d_c = _causal_dwconv(cconv_ext, cconv_w_ref, cconv_h0, CCONV_K, lc).reshape(tm, W_BR)
    d_ln = _layernorm(d_c + cconv_b_ref[...], cln_g_ref[...], cln_b_ref[...])
    d_out = d_ln * jax.nn.sigmoid(d_ln)
    carry(cconv_ext, cconv_h0, CCONV_K - 1, cconv_out_ref)

    c_uv = _gelu_tanh(c_uv)
    c_u, c_v = c_uv[:, :W_BR], c_uv[:, W_BR:]
    c_vn = _layernorm(c_v, sgu_g_ref[...], sgu_b_ref[...])
    if emit_v:
        v_out_ref[...] = c_vn.reshape(nb, lc, W_BR)
    row = lax.broadcasted_iota(jnp.int32, (CHUNK, CHUNK), 0)
    col = lax.broadcasted_iota(jnp.int32, (CHUNK, CHUNK), 1)
    keep = row >= col
    seg = min(lc, CHUNK)
    if seg < CHUNK:
        keep = keep & ((row // seg) == (col // seg))
    c_vn_bf = c_vn.astype(BF16)
    s_cols = []
    for g in range(N_SGU_GROUPS):
        sl = slice(g * SGU_GW, (g + 1) * SGU_GW)
        w_g = jnp.where(keep, sgu_w_ref[g], 0.0).astype(BF16)
        bias = sgu_bcol_ref[:, g:g + 1]
        s_rows = [_dot(w_g, c_vn_bf[c * CHUNK:(c + 1) * CHUNK, sl]) + bias
                  for c in range(tm // CHUNK)]
        s_cols.append(jnp.concatenate(s_rows, axis=0))
    c_out = c_u * jnp.concatenate(s_cols, axis=-1)

    b_h, b_b, b_c = z_b[:, :W_BR], z_b[:, W_BR:2 * W_BR], z_b[:, 2 * W_BR:]
    _store_planes(sconv_ext, SCONV_OFF, (b_c * b_h).reshape(nb, lc, W_BR))
    conv = _causal_dwconv(sconv_ext, sconv_w_ref, sconv_h0, SCONV_K, lc)
    b_out = b_b * conv.reshape(tm, W_BR)
    carry(sconv_ext, sconv_h0, SCONV_K - 1, sconv_out_ref)

    _store_planes(pool_ext, POOL_OFF, a_u.reshape(nb, lc, W_BR))
    pos = pos0 + j * lc + lax.broadcasted_iota(jnp.int32, (1, lc, POOL_GW), 1)
    a_parts = []
    for g, w in enumerate(POOL_WINDOWS):
        cur = pool_ext[g, :, POOL_OFF:POOL_OFF + lc, :]
        win = cur
        for i in range(1, w):
            win = win + pool_ext[g, :, POOL_OFF - i:POOL_OFF - i + lc, :]
        cnt = jnp.minimum(pos + 1, w).astype(F32)
        p = win / cnt - cur
        a_parts.append(_dot(p.reshape(tm, POOL_GW).astype(BF16), pool_w_ref[g]))
    a_out = jnp.concatenate(a_parts, axis=-1) * pool_scale_ref[...]
    carry(pool_ext, pool_h0, POOL_PREV, pool_out_ref)

    merged = None
    for k, branch in enumerate((a_out, b_out, c_out, d_out)):
        term = gates[k] * _dot(branch.astype(BF16), w_br_ref[k])
        merged = term if merged is None else merged + term

    y = x + _dot(merged.astype(BF16), w_mix_ref[...])
    y_ref[...] = y.reshape(nb, lc, D_MODEL)


def _mixer(x, pool_st, sconv_st, cconv_st, layer, stacked, lw, *, nb, lc, pos0, emit_v):
    n_seq, seq_len, _ = x.shape
    assert n_seq % nb == 0 and seq_len % lc == 0 and (nb * lc) % CHUNK == 0
    assert lc % SUBLANES == 0 and (lc % CHUNK == 0 or CHUNK % lc == 0)
    assert lc == seq_len or lc >= CCONV_K - 1
    nj = seq_len // lc
    grid = (n_seq // nb, nj)

    def state_tile(rows):
        return pl.BlockSpec((None, nb, rows, W_BR), lambda b, j: (layer, b, 0, 0))

    seq_tile = pl.BlockSpec((nb, lc, D_MODEL), lambda b, j: (b, j, 0))
    params = (lw['g_mix'], lw['w_in'], lw['pool_w'], lw['pool_scale'], lw['sconv_w'],
              lw['sgu_ln_g'], lw['sgu_ln_b'], lw['sgu_w'], lw['sgu_bcol'],
              lw['cconv_w'], lw['cconv_b'], lw['cconv_ln_g'], lw['cconv_ln_b'],
              lw['w_branch'], lw['b_gate'], lw['w_mix_out'])
    in_specs = [seq_tile, state_tile(POOL_PREV), state_tile(SCONV_K - 1), state_tile(CCONV_K - 1)]
    in_specs += [_layer_resident(p, layer) for p in params]
    in_specs += [pl.BlockSpec(memory_space=pl.ANY)] * len(stacked)
    out_shape = [jax.ShapeDtypeStruct(x.shape, F32)]
    out_shape += [jax.ShapeDtypeStruct(s.shape, s.dtype) for s in stacked]
    out_specs = [seq_tile, state_tile(POOL_PREV), state_tile(SCONV_K - 1), state_tile(CCONV_K - 1)]
    if emit_v:
        out_specs.append(pl.BlockSpec((None, nb, lc, W_BR), lambda b, j: (layer, b, j, 0)))
    assert len(stacked) == len(out_specs) - 1
    first_stacked_in = 4 + len(params)
    outs = pl.pallas_call(
        functools.partial(_mixer_kernel, nb=nb, lc=lc, nj=nj, pos0=pos0, emit_v=emit_v,
                          n_aliased=len(stacked)),
        grid=grid, in_specs=in_specs, out_specs=out_specs, out_shape=out_shape,
        input_output_aliases={first_stacked_in + i: 1 + i for i in range(len(stacked))},
        scratch_shapes=[pltpu.VMEM((N_PLANES, nb, POOL_OFF + lc, LANES), F32),
                        pltpu.VMEM((N_PLANES, nb, SCONV_OFF + lc, LANES), F32),
                        pltpu.VMEM((N_PLANES, nb, CCONV_OFF + lc, LANES), F32)],
        compiler_params=pltpu.CompilerParams(
            dimension_semantics=("arbitrary", "arbitrary"),
            vmem_limit_bytes=VMEM_LIMIT_BYTES),
        name="mixer",
    )(x, pool_st, sconv_st, cconv_st, *params, *stacked)
    return outs[0], tuple(outs[1:])


def _attn_body(x_ref, klo_ref, khi_ref, vlo_ref, vhi_ref, g_ref, wq_ref, wo_ref, y_ref,
               k_buf, v_buf, o_buf, *, nb, lq, new_memory):
    half = XHEAD_DIM // 2

    def _gather_heads():
        for lo_ref, hi_ref, buf in ((klo_ref, khi_ref, k_buf), (vlo_ref, vhi_ref, v_buf)):
            lo = lo_ref.reshape(nb * N_MEM * N_XHEADS, half)
            hi = hi_ref.reshape(nb * N_MEM * N_XHEADS, half)
            for b in range(nb):
                for hd in range(N_XHEADS):
                    rows = pl.ds(b * N_MEM * N_XHEADS + hd, N_MEM, stride=N_XHEADS)
                    c0 = hd * XHEAD_DIM
                    buf[b, :, c0:c0 + half] = lo[rows, :].astype(BF16)
                    buf[b, :, c0 + half:c0 + XHEAD_DIM] = hi[rows, :].astype(BF16)

    if new_memory is None:
        _gather_heads()
    else:
        pl.when(new_memory)(_gather_heads)

    x = x_ref[...].reshape(nb * lq, D_MODEL)
    h = _rmsnorm(x, g_ref[...]).astype(BF16)
    q = _dot(h, wq_ref[...])
    pairs = [(b, hd) for b in range(nb) for hd in range(N_XHEADS)]
    cols = lambda hd: slice(hd * XHEAD_DIM, (hd + 1) * XHEAD_DIM)
    s = jnp.concatenate(
        [lax.dot_general(q[b * lq:(b + 1) * lq, cols(hd)].astype(BF16), k_buf[b, :, cols(hd)],
                         (((1,), (1,)), ((), ())), preferred_element_type=F32)
         for b, hd in pairs], axis=0) * (XHEAD_DIM ** -0.5)
    e = jnp.exp(s - jnp.max(s, axis=-1, keepdims=True))
    p = e / jnp.sum(e, axis=-1, keepdims=True)
    for i, (b, hd) in enumerate(pairs):
        o_buf[b * lq:(b + 1) * lq, cols(hd)] = _dot(p[i * lq:(i + 1) * lq].astype(BF16),
                                                    v_buf[b, :, cols(hd)])
    y = x + _dot(o_buf[...].astype(BF16), wo_ref[...])
    y_ref[...] = y.reshape(nb, lq, D_MODEL)


N_ATTN_INPUTS = 8


def _attn_kernel(*refs, nb, lq, n_cast):
    ins, refs = refs[:N_ATTN_INPUTS], refs[N_ATTN_INPUTS:]
    cast_in, refs = refs[:n_cast], refs[n_cast:]
    y_ref, refs = refs[0], refs[1:]
    cast_out, scratch = refs[:n_cast], refs[n_cast:]
    _attn_body(*ins, y_ref, *scratch, nb=nb, lq=lq, new_memory=pl.program_id(1) == 0)
    _cast_blocks(cast_in, cast_out)


def _attn(x, mem_k, mem_v, layer, g, w_q, w_o, *, nb, lq, also_cast=()):
    n_seq, seq_len, _ = x.shape
    assert n_seq % nb == 0 and seq_len % lq == 0 and lq % SUBLANES == 0
    assert XHEAD_DIM == 2 * LANES
    nj = seq_len // lq
    steps = (n_seq // nb) * nj
    seq_tile = pl.BlockSpec((nb, lq, D_MODEL), lambda b, j: (b, j, 0))

    def mem_half(part):
        return pl.BlockSpec((None, nb, N_MEM, N_XHEADS, LANES),
                            lambda b, j: (layer, b, 0, 0, part))

    cast_in_specs, cast_out_specs, cast_shapes = _side_casts(also_cast, steps,
                                                             lambda b, j: b * nj + j)
    outs = pl.pallas_call(
        functools.partial(_attn_kernel, nb=nb, lq=lq, n_cast=len(also_cast)),
        grid=(n_seq // nb, nj),
        in_specs=[seq_tile, mem_half(0), mem_half(1), mem_half(0), mem_half(1),
                  _layer_resident(g, layer), _layer_resident(w_q, layer),
                  _layer_resident(w_o, layer)] + cast_in_specs,
        out_specs=[seq_tile] + cast_out_specs,
        out_shape=[jax.ShapeDtypeStruct(x.shape, F32)] + cast_shapes,
        scratch_shapes=[pltpu.VMEM((nb, N_MEM, D_MODEL), BF16),
                        pltpu.VMEM((nb, N_MEM, D_MODEL), BF16),
                        pltpu.VMEM((nb * lq, D_MODEL), F32)],
        compiler_params=pltpu.CompilerParams(
            dimension_semantics=("arbitrary", "arbitrary"),
            vmem_limit_bytes=VMEM_LIMIT_BYTES),
        name="cross_attn",
    )(x, mem_k, mem_k, mem_v, mem_v, g, w_q, w_o, *(a for a, _, _ in also_cast))
    return outs[0], tuple(outs[1:])


def _ffn_kernel(x_ref, g_ref, w1_ref, w2_ref, gf_ref, y_ref, *, final_norm):
    x = x_ref[...]
    h = _rmsnorm(x, g_ref[...]).astype(BF16)
    y = x
    for c in range(D_FF // D_MODEL):
        u = jnp.square(jnp.maximum(_dot(h, w1_ref[:, c * D_MODEL:(c + 1) * D_MODEL]), 0.0))
        y = y + _dot(u.astype(BF16), w2_ref[c * D_MODEL:(c + 1) * D_MODEL, :])
    if final_norm:
        y = _rmsnorm(y, gf_ref[...])
    y_ref[...] = y


def _ffn(x, layer, g, w1, w2, g_final, *, tm, final_norm):
    t = x.shape[0]
    assert t % tm == 0
    row_tile = pl.BlockSpec((tm, D_MODEL), lambda i: (i, 0))
    return pl.pallas_call(
        functools.partial(_ffn_kernel, final_norm=final_norm),
        grid=(t // tm,),
        in_specs=[row_tile, _layer_resident(g, layer), _layer_resident(w1, layer),
                  _layer_resident(w2, layer), _resident(g_final.shape)],
        out_specs=row_tile,
        out_shape=jax.ShapeDtypeStruct(x.shape, F32),
        compiler_params=pltpu.CompilerParams(
            dimension_semantics=("arbitrary",), vmem_limit_bytes=VMEM_LIMIT_BYTES),
        name="ffn",
    )(x, g, w1, w2, g_final)


def _ffn_and_attn_kernel(xf_ref, gf_ref, w1_ref, w2_ref, gfin_ref,
                         xa_ref, klo_ref, khi_ref, vlo_ref, vhi_ref, ga_ref, wq_ref, wo_ref,
                         yf_ref, ya_ref, k_buf, v_buf, o_buf, *, final_norm, nb, lq):
    _ffn_kernel(xf_ref, gf_ref, w1_ref, w2_ref, gfin_ref, yf_ref, final_norm=final_norm)
    _attn_body(xa_ref, klo_ref, khi_ref, vlo_ref, vhi_ref, ga_ref, wq_ref, wo_ref, ya_ref,
               k_buf, v_buf, o_buf, nb=nb, lq=lq, new_memory=None)


def _ffn_and_attn(x_ffn, x_attn, mem_k, mem_v, layer, lw, g_final, *, tm, final_norm):
    t = x_ffn.shape[0]
    n_seq, lq, _ = x_attn.shape
    assert t % tm == 0 and n_seq % (t // tm) == 0 and lq % SUBLANES == 0
    steps = t // tm
    nb = n_seq // steps
    row_tile = pl.BlockSpec((tm, D_MODEL), lambda i: (i, 0))
    seq_tile = pl.BlockSpec((nb, lq, D_MODEL), lambda i: (i, 0, 0))

    def mem_half(part):
        return pl.BlockSpec((None, nb, N_MEM, N_XHEADS, LANES), lambda i: (layer, i, 0, 0, part))

    return pl.pallas_call(
        functools.partial(_ffn_and_attn_kernel, final_norm=final_norm, nb=nb, lq=lq),
        grid=(steps,),
        in_specs=[row_tile, _layer_resident(lw['g_ffn'], layer), _layer_resident(lw['w_ff1'], layer),
                  _layer_resident(lw['w_ff2'], layer), _resident(g_final.shape),
                  seq_tile, mem_half(0), mem_half(1), mem_half(0), mem_half(1),
                  _layer_resident(lw['g_x'], layer), _layer_resident(lw['w_xq'], layer),
                  _layer_resident(lw['w_xo'], layer)],
        out_specs=[row_tile, seq_tile],
        out_shape=[jax.ShapeDtypeStruct(x_ffn.shape, F32), jax.ShapeDtypeStruct(x_attn.shape, F32)],
        scratch_shapes=[pltpu.VMEM((nb, N_MEM, D_MODEL), BF16),
                        pltpu.VMEM((nb, N_MEM, D_MODEL), BF16),
                        pltpu.VMEM((nb * lq, D_MODEL), F32)],
        compiler_params=pltpu.CompilerParams(
            dimension_semantics=("arbitrary",), vmem_limit_bytes=VMEM_LIMIT_BYTES),
        name="ffn_and_attn",
    )(x_ffn, lw['g_ffn'], lw['w_ff1'], lw['w_ff2'], g_final,
      x_attn, mem_k, mem_k, mem_v, mem_v, lw['g_x'], lw['w_xq'], lw['w_xo'])


def _mem_kv_kernel(m_ref, g_ref, *refs, nb, n_cast):
    wk_refs, wv_refs = refs[:N_XHEADS], refs[N_XHEADS:2 * N_XHEADS]
    cast_in, refs = refs[2 * N_XHEADS:2 * N_XHEADS + n_cast], refs[2 * N_XHEADS + n_cast:]
    k_ref, v_ref, *cast_out = refs
    _cast_blocks(cast_in, cast_out)
    m = _rmsnorm(m_ref[...], g_ref[...]).astype(BF16)
    for w_refs, o_ref in ((wk_refs, k_ref), (wv_refs, v_ref)):
        cols = jnp.concatenate([r[...].astype(BF16) for r in w_refs], axis=-1)
        y = _dot(m, cols)
        flat = o_ref.reshape(nb * N_MEM * N_XHEADS, LANES)
        for b in range(nb):
            for hd in range(N_XHEADS):
                rows = pl.ds(b * N_MEM * N_XHEADS + hd, N_MEM, stride=N_XHEADS)
                flat[rows, :] = y[b * N_MEM:(b + 1) * N_MEM, hd * LANES:(hd + 1) * LANES]


def _mem_kv(mem, g, w_k, w_v, *, nb, also_cast=()):
    n_seq = mem.shape[0]
    assert n_seq % nb == 0 and XHEAD_DIM == 2 * LANES
    rows = nb * N_MEM
    n_i = n_seq // nb
    shape = (DEPTH, n_seq, N_MEM, N_XHEADS, XHEAD_DIM)
    cast_in_specs, cast_out_specs, cast_shapes = _side_casts(
        also_cast, DEPTH * 2 * n_i, lambda l, part, i: (l * 2 + part) * n_i + i)

    def head_half_cols(hd):
        return pl.BlockSpec((None, D_MODEL, LANES), lambda l, part, i: (l, 0, 2 * hd + part))

    w_specs = [head_half_cols(hd) for hd in range(N_XHEADS)]
    out_half = pl.BlockSpec((None, nb, N_MEM, N_XHEADS, LANES), lambda l, part, i: (l, i, 0, 0, part))
    outs = pl.pallas_call(
        functools.partial(_mem_kv_kernel, nb=nb, n_cast=len(also_cast)),
        grid=(DEPTH, 2, n_i),
        in_specs=[pl.BlockSpec((rows, D_MODEL), lambda l, part, i: (i, 0)),
                  pl.BlockSpec((None, 1, D_MODEL), lambda l, part, i: (l, 0, 0))]
                 + w_specs + w_specs + cast_in_specs,
        out_specs=[out_half] * 2 + cast_out_specs,
        out_shape=[jax.ShapeDtypeStruct(shape, F32)] * 2 + cast_shapes,
        compiler_params=pltpu.CompilerParams(
            dimension_semantics=("arbitrary", "arbitrary", "arbitrary"),
            vmem_limit_bytes=VMEM_LIMIT_BYTES),
        name="mem_kv",
    )(mem.reshape(n_seq * N_MEM, D_MODEL), g, *([w_k] * N_XHEADS), *([w_v] * N_XHEADS),
      *(a for a, _, _ in also_cast))
    return outs[0], outs[1], tuple(outs[2:])


def kernel(x_prompt, x_sample, state_pool, state_sconv, state_cconv, cache_mem_k, cache_mem_v, mem_prompt, norm_mix_g, w_in, pool_w, pool_scale, sconv_w, sgu_ln_g, sgu_ln_b, sgu_w, sgu_b, cconv_w, cconv_b, cconv_ln_g, cconv_ln_b, w_branch, b_gate, w_mix_out, norm_x_g, norm_mem_g, w_xq, w_xk, w_xv, w_xo, norm_ffn_g, w_ff1, w_ff2, norm_f_g):
    batch, seq, _ = x_prompt.shape
    dec_batch, dec_seq, _ = x_sample.shape

    def rows(v):
        return v.reshape(DEPTH, 1, -1)

    reps = CHUNK // dec_seq
    small = {
        'g_mix': rows(norm_mix_g), 'pool_w': pool_w.astype(BF16), 'pool_scale': rows(pool_scale),
        'sconv_w': sconv_w, 'sgu_ln_g': rows(sgu_ln_g), 'sgu_ln_b': rows(sgu_ln_b),
        'cconv_w': cconv_w, 'cconv_b': rows(cconv_b),
        'cconv_ln_g': rows(cconv_ln_g), 'cconv_ln_b': rows(cconv_ln_b),
        'b_gate': b_gate, 'g_x': rows(norm_x_g), 'g_ffn': rows(norm_ffn_g),
    }
    sgu_p = {'sgu_w': sgu_w, 'sgu_bcol': jnp.swapaxes(sgu_b, 1, 2)}
    sgu_s = {'sgu_w': jnp.tile(sgu_w[:, :, :dec_seq, :dec_seq], (1, 1, reps, reps)),
             'sgu_bcol': jnp.swapaxes(jnp.tile(sgu_b[:, :, :dec_seq], (1, 1, reps)), 1, 2)}
    big = {'w_in': w_in, 'w_branch': w_branch, 'w_mix_out': w_mix_out, 'w_xq': w_xq, 'w_xo': w_xo}
    as_rows = lambda w: w.reshape(-1, w.shape[-1])
    per_layer = lambda w: as_rows(w).shape[0] // DEPTH
    first = [(as_rows(w), 0, per_layer(w)) for w in big.values()]
    later = [(as_rows(w_ff1), 0, DEPTH * D_MODEL), (as_rows(w_ff2), 0, DEPTH * D_FF)]
    later += [(as_rows(w), per_layer(w), per_layer(w)) for w in big.values()]
    one_layer = lambda cast: {k: c.reshape((1,) + w.shape[1:]) for (k, w), c in zip(big.items(), cast)}
    g_final = norm_f_g.reshape(1, -1)

    mem_k_prompt, mem_v_prompt, cast = _mem_kv(mem_prompt, rows(norm_mem_g), w_xk, w_xv, nb=4,
                                               also_cast=first)
    weights = one_layer(cast)
    zeros = lambda n_seq, n: jnp.zeros((DEPTH, n_seq, n, W_BR), F32)
    hist_p = (zeros(batch, POOL_PREV), zeros(batch, SCONV_K - 1), zeros(batch, CCONV_K - 1))
    hist_s = (state_pool, state_sconv, state_cconv)
    new_p = tuple(jnp.zeros_like(s) for s in hist_p)
    new_s = tuple(jnp.zeros_like(s) for s in hist_s) + (zeros(dec_batch, dec_seq),)
    x_p, x_s = x_prompt, x_sample
    for l in range(DEPTH):
        last = l == DEPTH - 1
        lw_p, lw_s = dict(small, **sgu_p, **weights), dict(small, **sgu_s, **weights)
        x_p, new_p = _mixer(x_p, *hist_p, l, new_p, lw_p, nb=1, lc=512, pos0=0, emit_v=False)
        x_p, cast = _attn(x_p, mem_k_prompt, mem_v_prompt, l, lw_p['g_x'], lw_p['w_xq'], lw_p['w_xo'],
                          nb=1, lq=1024, also_cast=() if l else later)
        if cast:
            ffn_w = {'w_ff1': cast[0].reshape(w_ff1.shape), 'w_ff2': cast[1].reshape(w_ff2.shape)}
            next_weights = one_layer(cast[2:])
        x_s, new_s = _mixer(x_s, *hist_s, l, new_s, lw_s, nb=256 // dec_seq, lc=dec_seq,
                            pos0=PAST_LEN, emit_v=True)
        lw_ffn = dict(lw_p, **ffn_w)
        x_p, x_s = _ffn_and_attn(x_p.reshape(batch * seq, D_MODEL), x_s, cache_mem_k, cache_mem_v,
                                 l, lw_ffn, g_final, tm=512, final_norm=last)
        x_p = x_p.reshape(batch, seq, D_MODEL)
        x_s = _ffn(x_s.reshape(dec_batch * dec_seq, D_MODEL), l, lw_ffn['g_ffn'], lw_ffn['w_ff1'],
                   lw_ffn['w_ff2'], g_final, tm=512, final_norm=last).reshape(x_sample.shape)
        weights = next_weights

    return (x_p, x_s) + new_p + (mem_k_prompt, mem_v_prompt) + new_s
```

```python
import functools

import jax
import jax.numpy as jnp
from jax import lax
from jax.experimental import pallas as pl
from jax.experimental.pallas import tpu as pltpu

D_MODEL = 1024
DEPTH = 2
PAST_LEN = 16384
W_BR = D_MODEL // 2
POOL_WINDOWS = (2, 4, 8, 16)
POOL_GW = W_BR // len(POOL_WINDOWS)
POOL_PREV = max(POOL_WINDOWS) - 1
SCONV_K = 3
CCONV_K = 31
CHUNK = 128
N_SGU_GROUPS = 4
SGU_GW = W_BR // N_SGU_GROUPS
N_MEM = 256
N_XHEADS = 4
XHEAD_DIM = D_MODEL // N_XHEADS
D_FF = 4 * D_MODEL
N_BRANCH = 4
EPS = 1e-6
GATE_COL0 = 8 * W_BR

SUBLANES = 8
LANES = 128
N_PLANES = W_BR // LANES
POOL_OFF = 16
SCONV_OFF = 8
CCONV_OFF = 32

VMEM_LIMIT_BYTES = 56 * 1024 * 1024

BF16 = jnp.bfloat16
F32 = jnp.float32


def _dot(a, b):
    return jnp.dot(a, b, preferred_element_type=F32)


def _wdot(a, w_packed):
    return _dot(a, pltpu.bitcast(w_packed, BF16))


def _rmsnorm(x, g):
    return x * lax.rsqrt(jnp.mean(x * x, axis=-1, keepdims=True) + EPS) * g


def _layernorm(x, g, b):
    xc = x - jnp.mean(x, axis=-1, keepdims=True)
    var = jnp.mean(xc * xc, axis=-1, keepdims=True)
    return xc * lax.rsqrt(var + EPS) * g + b


def _gelu_tanh(x):
    return x * (0.5 * (1.0 + jnp.tanh(0.7978845608028654 * (x + 0.044715 * (x * x * x)))))


def _resident(shape):
    nd = len(shape)
    return pl.BlockSpec(shape, lambda *_: (0,) * nd, pipeline_mode=pl.Buffered(1))


def _layer_resident(stacked, layer):
    index = (layer if stacked.shape[0] > 1 else 0,) + (0,) * (stacked.ndim - 1)
    return pl.BlockSpec((None,) + stacked.shape[1:], lambda *_: index,
                        pipeline_mode=pl.Buffered(1))


def _side_casts(jobs, steps, step_of):
    def row_block(a, row0, n_rows, pack):
        assert a.ndim == 2 and n_rows % (steps * 2 * SUBLANES) == 0
        rows = n_rows // steps
        assert row0 % rows == 0
        return pl.BlockSpec((rows // pack, a.shape[1]),
                            lambda *ids: (row0 // rows + step_of(*ids), 0))

    return ([row_block(a, row0, n_rows, 1) for a, row0, n_rows in jobs],
            [row_block(a, 0, n_rows, 2) for a, _, n_rows in jobs],
            [jax.ShapeDtypeStruct((n_rows // 2, a.shape[1]), jnp.uint32) for a, _, n_rows in jobs])


def _cast_blocks(src_refs, dst_refs):
    for src_ref, dst_ref in zip(src_refs, dst_refs):
        dst_ref[...] = pltpu.bitcast(src_ref[...].astype(BF16), jnp.uint32)


def _store_planes(ext, row0, value):
    rows = value.shape[1]
    for c in range(N_PLANES):
        ext[c, :, row0:row0 + rows, :] = value[:, :, c * LANES:(c + 1) * LANES]


def _load_planes(ext, row0, rows):
    return jnp.concatenate([ext[c, :, row0:row0 + rows, :] for c in range(N_PLANES)], axis=-1)


def _causal_dwconv(ext, w_ref, row0, taps, rows):
    planes = []
    for c in range(N_PLANES):
        cs = slice(c * LANES, (c + 1) * LANES)
        acc = w_ref[0:1, cs][None] * ext[c, :, row0:row0 + rows, :]
        for k in range(1, taps):
            acc = acc + w_ref[k:k + 1, cs][None] * ext[c, :, row0 + k:row0 + k + rows, :]
        planes.append(acc)
    return jnp.concatenate(planes, axis=-1)


def _mixer_kernel(x_ref, pool_st_ref, sconv_st_ref, cconv_st_ref,
                  g_mix_ref, w_in_ref, pool_w_ref, pool_scale_ref, sconv_w_ref,
                  sgu_g_ref, sgu_b_ref, sgu_w_ref, sgu_bcol_ref,
                  cconv_w_ref, cconv_b_ref, cln_g_ref, cln_b_ref,
                  w_br_ref, b_gate_ref, w_mix_ref, *rest,
                  nb, lc, nj, pos0, emit_v, n_aliased):
    y_ref, pool_out_ref, sconv_out_ref, cconv_out_ref, *rest = rest[n_aliased:]
    if emit_v:
        v_out_ref, pool_ext, sconv_ext, cconv_ext = rest
    else:
        pool_ext, sconv_ext, cconv_ext = rest
    j = pl.program_id(1)
    tm = nb * lc
    pool_h0 = POOL_OFF - POOL_PREV
    sconv_h0 = SCONV_OFF - (SCONV_K - 1)
    cconv_h0 = CCONV_OFF - (CCONV_K - 1)

    x = x_ref[...].reshape(tm, D_MODEL)
    h = _rmsnorm(x, g_mix_ref[...]).astype(BF16)

    @pl.when(j == 0)
    def _load_history():
        _store_planes(pool_ext, pool_h0, pool_st_ref[...])
        _store_planes(sconv_ext, sconv_h0, sconv_st_ref[...])
        _store_planes(cconv_ext, cconv_h0, cconv_st_ref[...])

    def carry(ext, h0, prev, out_ref):
        new = _load_planes(ext, lc + h0, prev)
        out_ref[...] = new
        if nj > 1:
            _store_planes(ext, h0, new)

    d_in = _wdot(h, w_in_ref[:, 6 * W_BR:8 * W_BR])
    c_uv = _wdot(h, w_in_ref[:, 4 * W_BR:6 * W_BR])
    z_b = _wdot(h, w_in_ref[:, W_BR:4 * W_BR])
    a_u = _wdot(h, w_in_ref[:, 0:W_BR])
    gates = [jax.nn.sigmoid(_wdot(h, w_in_ref[:, GATE_COL0 + k * D_MODEL:GATE_COL0 + (k + 1) * D_MODEL])
                            + b_gate_ref[k:k + 1, :]) for k in range(N_BRANCH)]

    glu = d_in[:, :W_BR] * jax.nn.sigmoid(d_in[:, W_BR:])
    _store_planes(cconv_ext, CCONV_OFF, glu.reshape(nb, lc, W_BR))
    d_c = _causal_dwconv(cconv_ext, cconv_w_ref, cconv_h0, CCONV_K, lc).reshape(tm, W_BR)
    d_ln = _layernorm(d_c + cconv_b_ref[...], cln_g_ref[...], cln_b_ref[...])
    d_out = d_ln * jax.nn.sigmoid(d_ln)
    carry(cconv_ext, cconv_h0, CCONV_K - 1, cconv_out_ref)

    c_uv = _gelu_tanh(c_uv)
    c_u, c_v = c_uv[:, :W_BR], c_uv[:, W_BR:]
    c_vn = _layernorm(c_v, sgu_g_ref[...], sgu_b_ref[...])
    if emit_v:
        v_out_ref[...] = c_vn.reshape(nb, lc, W_BR)
    row = lax.broadcasted_iota(jnp.int32, (CHUNK, CHUNK), 0)
    col = lax.broadcasted_iota(jnp.int32, (CHUNK, CHUNK), 1)
    keep = row >= col
    seg = min(lc, CHUNK)
    if seg < CHUNK:
        keep = keep & ((row // seg) == (col // seg))
    c_vn_bf = c_vn.astype(BF16)
    s_cols = []
    for g in range(N_SGU_GROUPS):
        sl = slice(g * SGU_GW, (g + 1) * SGU_GW)
        w_g = jnp.where(keep, sgu_w_ref[g], 0.0).astype(BF16)
        bias = sgu_bcol_ref[:, g:g + 1]
        s_rows = [_dot(w_g, c_vn_bf[c * CHUNK:(c + 1) * CHUNK, sl]) + bias
                  for c in range(tm // CHUNK)]
        s_cols.append(jnp.concatenate(s_rows, axis=0))
    c_out = c_u * jnp.concatenate(s_cols, axis=-1)

    b_h, b_b, b_c = z_b[:, :W_BR], z_b[:, W_BR:2 * W_BR], z_b[:, 2 * W_BR:]
    _store_planes(sconv_ext, SCONV_OFF, (b_c * b_h).reshape(nb, lc, W_BR))
    conv = _causal_dwconv(sconv_ext, sconv_w_ref, sconv_h0, SCONV_K, lc)
    b_out = b_b * conv.reshape(tm, W_BR)
    carry(sconv_ext, sconv_h0, SCONV_K - 1, sconv_out_ref)

    _store_planes(pool_ext, POOL_OFF, a_u.reshape(nb, lc, W_BR))
    pos = pos0 + j * lc + lax.broadcasted_iota(jnp.int32, (1, lc, POOL_GW), 1)
    a_parts = []
    for g, w in enumerate(POOL_WINDOWS):
        cur = pool_ext[g, :, POOL_OFF:POOL_OFF + lc, :]
        win = cur
        for i in range(1, w):
            win = win + pool_ext[g, :, POOL_OFF - i:POOL_OFF - i + lc, :]
        cnt = jnp.minimum(pos + 1, w).astype(F32)
        p = win / cnt - cur
        a_parts.append(_dot(p.reshape(tm, POOL_GW).astype(BF16), pool_w_ref[g]))
    a_out = jnp.concatenate(a_parts, axis=-1) * pool_scale_ref[...]
    carry(pool_ext, pool_h0, POOL_PREV, pool_out_ref)

    merged = None
    for k, branch in enumerate((a_out, b_out, c_out, d_out)):
        term = gates[k] * _wdot(branch.astype(BF16), w_br_ref[k])
        merged = term if merged is None else merged + term

    y = x + _wdot(merged.astype(BF16), w_mix_ref[...])
    y_ref[...] = y.reshape(nb, lc, D_MODEL)


def _mixer(x, pool_st, sconv_st, cconv_st, layer, stacked, lw, *, nb, lc, pos0, emit_v):
    n_seq, seq_len, _ = x.shape
    assert n_seq % nb == 0 and seq_len % lc == 0 and (nb * lc) % CHUNK == 0
    assert lc % SUBLANES == 0 and (lc % CHUNK == 0 or CHUNK % lc == 0)
    assert lc == seq_len or lc >= CCONV_K - 1
    nj = seq_len // lc
    grid = (n_seq // nb, nj)

    def state_tile(rows):
        return pl.BlockSpec((None, nb, rows, W_BR), lambda b, j: (layer, b, 0, 0))

    seq_tile = pl.BlockSpec((nb, lc, D_MODEL), lambda b, j: (b, j, 0))
    params = (lw['g_mix'], lw['w_in'], lw['pool_w'], lw['pool_scale'], lw['sconv_w'],
              lw['sgu_ln_g'], lw['sgu_ln_b'], lw['sgu_w'], lw['sgu_bcol'],
              lw['cconv_w'], lw['cconv_b'], lw['cconv_ln_g'], lw['cconv_ln_b'],
              lw['w_branch'], lw['b_gate'], lw['w_mix_out'])
    in_specs = [seq_tile, state_tile(POOL_PREV), state_tile(SCONV_K - 1), state_tile(CCONV_K - 1)]
    in_specs += [_layer_resident(p, layer) for p in params]
    in_specs += [pl.BlockSpec(memory_space=pl.ANY)] * len(stacked)
    out_shape = [jax.ShapeDtypeStruct(x.shape, F32)]
    out_shape += [jax.ShapeDtypeStruct(s.shape, s.dtype) for s in stacked]
    out_specs = [seq_tile, state_tile(POOL_PREV), state_tile(SCONV_K - 1), state_tile(CCONV_K - 1)]
    if emit_v:
        out_specs.append(pl.BlockSpec((None, nb, lc, W_BR), lambda b, j: (layer, b, j, 0)))
    assert len(stacked) == len(out_specs) - 1
    first_stacked_in = 4 + len(params)
    outs = pl.pallas_call(
        functools.partial(_mixer_kernel, nb=nb, lc=lc, nj=nj, pos0=pos0, emit_v=emit_v,
                          n_aliased=len(stacked)),
        grid=grid, in_specs=in_specs, out_specs=out_specs, out_shape=out_shape,
        input_output_aliases={first_stacked_in + i: 1 + i for i in range(len(stacked))},
        scratch_shapes=[pltpu.VMEM((N_PLANES, nb, POOL_OFF + lc, LANES), F32),
                        pltpu.VMEM((N_PLANES, nb, SCONV_OFF + lc, LANES), F32),
                        pltpu.VMEM((N_PLANES, nb, CCONV_OFF + lc, LANES), F32)],
        compiler_params=pltpu.CompilerParams(
            dimension_semantics=("arbitrary", "arbitrary"),
            vmem_limit_bytes=VMEM_LIMIT_BYTES),
        name="mixer",
    )(x, pool_st, sconv_st, cconv_st, *params, *stacked)
    return outs[0], tuple(outs[1:])


def _attn_body(x_ref, klo_ref, khi_ref, vlo_ref, vhi_ref, g_ref, wq_ref, wo_ref, y_ref,
               k_buf, v_buf, o_buf, *, nb, lq, new_memory):
    half = XHEAD_DIM // 2

    def _gather_heads():
        for lo_ref, hi_ref, buf in ((klo_ref, khi_ref, k_buf), (vlo_ref, vhi_ref, v_buf)):
            lo = lo_ref.reshape(nb * N_MEM * N_XHEADS, half)
            hi = hi_ref.reshape(nb * N_MEM * N_XHEADS, half)
            for b in range(nb):
                for hd in range(N_XHEADS):
                    rows = pl.ds(b * N_MEM * N_XHEADS + hd, N_MEM, stride=N_XHEADS)
                    c0 = hd * XHEAD_DIM
                    buf[b, :, c0:c0 + half] = lo[rows, :].astype(BF16)
                    buf[b, :, c0 + half:c0 + XHEAD_DIM] = hi[rows, :].astype(BF16)

    if new_memory is None:
        _gather_heads()
    else:
        pl.when(new_memory)(_gather_heads)

    x = x_ref[...].reshape(nb * lq, D_MODEL)
    h = _rmsnorm(x, g_ref[...]).astype(BF16)
    q = _wdot(h, wq_ref[...])
    pairs = [(b, hd) for b in range(nb) for hd in range(N_XHEADS)]
    cols = lambda hd: slice(hd * XHEAD_DIM, (hd + 1) * XHEAD_DIM)
    s = jnp.concatenate(
        [lax.dot_general(q[b * lq:(b + 1) * lq, cols(hd)].astype(BF16), k_buf[b, :, cols(hd)],
                         (((1,), (1,)), ((), ())), preferred_element_type=F32)
         for b, hd in pairs], axis=0) * (XHEAD_DIM ** -0.5)
    e = jnp.exp(s - jnp.max(s, axis=-1, keepdims=True))
    p = e / jnp.sum(e, axis=-1, keepdims=True)
    for i, (b, hd) in enumerate(pairs):
        o_buf[b * lq:(b + 1) * lq, cols(hd)] = _dot(p[i * lq:(i + 1) * lq].astype(BF16),
                                                    v_buf[b, :, cols(hd)])
    y = x + _wdot(o_buf[...].astype(BF16), wo_ref[...])
    y_ref[...] = y.reshape(nb, lq, D_MODEL)


N_ATTN_INPUTS = 8


def _attn_kernel(*refs, nb, lq, n_cast):
    ins, refs = refs[:N_ATTN_INPUTS], refs[N_ATTN_INPUTS:]
    cast_in, refs = refs[:n_cast], refs[n_cast:]
    y_ref, refs = refs[0], refs[1:]
    cast_out, scratch = refs[:n_cast], refs[n_cast:]
    _attn_body(*ins, y_ref, *scratch, nb=nb, lq=lq, new_memory=pl.program_id(1) == 0)
    _cast_blocks(cast_in, cast_out)


def _attn(x, mem_k, mem_v, layer, g, w_q, w_o, *, nb, lq, also_cast=()):
    n_seq, seq_len, _ = x.shape
    assert n_seq % nb == 0 and seq_len % lq == 0 and lq % SUBLANES == 0
    assert XHEAD_DIM == 2 * LANES
    nj = seq_len // lq
    steps = (n_seq // nb) * nj
    seq_tile = pl.BlockSpec((nb, lq, D_MODEL), lambda b, j: (b, j, 0))

    def mem_half(part):
        return pl.BlockSpec((None, nb, N_MEM, N_XHEADS, LANES),
                            lambda b, j: (layer, b, 0, 0, part))

    cast_in_specs, cast_out_specs, cast_shapes = _side_casts(also_cast, steps,
                                                             lambda b, j: b * nj + j)
    outs = pl.pallas_call(
        functools.partial(_attn_kernel, nb=nb, lq=lq, n_cast=len(also_cast)),
        grid=(n_seq // nb, nj),
        in_specs=[seq_tile, mem_half(0), mem_half(1), mem_half(0), mem_half(1),
                  _layer_resident(g, layer), _layer_resident(w_q, layer),
                  _layer_resident(w_o, layer)] + cast_in_specs,
        out_specs=[seq_tile] + cast_out_specs,
        out_shape=[jax.ShapeDtypeStruct(x.shape, F32)] + cast_shapes,
        scratch_shapes=[pltpu.VMEM((nb, N_MEM, D_MODEL), BF16),
                        pltpu.VMEM((nb, N_MEM, D_MODEL), BF16),
                        pltpu.VMEM((nb * lq, D_MODEL), F32)],
        compiler_params=pltpu.CompilerParams(
            dimension_semantics=("arbitrary", "arbitrary"),
            vmem_limit_bytes=VMEM_LIMIT_BYTES),
        name="cross_attn",
    )(x, mem_k, mem_k, mem_v, mem_v, g, w_q, w_o, *(a for a, _, _ in also_cast))
    return outs[0], tuple(outs[1:])


def _ffn_kernel(x_ref, g_ref, w1_ref, w2_ref, gf_ref, y_ref, *, final_norm):
    x = x_ref[...]
    h = _rmsnorm(x, g_ref[...]).astype(BF16)
    y = x
    for c in range(D_FF // D_MODEL):
        u = jnp.square(jnp.maximum(_wdot(h, w1_ref[:, c * D_MODEL:(c + 1) * D_MODEL]), 0.0))
        y = y + _wdot(u.astype(BF16), w2_ref[c * D_MODEL // 2:(c + 1) * D_MODEL // 2, :])
    if final_norm:
        y = _rmsnorm(y, gf_ref[...])
    y_ref[...] = y


def _ffn(x, layer, g, w1, w2, g_final, *, tm, final_norm):
    t = x.shape[0]
    assert t % tm == 0
    row_tile = pl.BlockSpec((tm, D_MODEL), lambda i: (i, 0))
    return pl.pallas_call(
        functools.partial(_ffn_kernel, final_norm=final_norm),
        grid=(t // tm,),
        in_specs=[row_tile, _layer_resident(g, layer), _layer_resident(w1, layer),
                  _layer_resident(w2, layer), _resident(g_final.shape)],
        out_specs=row_tile,
        out_shape=jax.ShapeDtypeStruct(x.shape, F32),
        compiler_params=pltpu.CompilerParams(
            dimension_semantics=("arbitrary",), vmem_limit_bytes=VMEM_LIMIT_BYTES),
        name="ffn",
    )(x, g, w1, w2, g_final)


def _ffn_and_attn_kernel(xf_ref, gf_ref, w1_ref, w2_ref, gfin_ref,
                         xa_ref, klo_ref, khi_ref, vlo_ref, vhi_ref, ga_ref, wq_ref, wo_ref,
                         yf_ref, ya_ref, k_buf, v_buf, o_buf, *, final_norm, nb, lq):
    _ffn_kernel(xf_ref, gf_ref, w1_ref, w2_ref, gfin_ref, yf_ref, final_norm=final_norm)
    _attn_body(xa_ref, klo_ref, khi_ref, vlo_ref, vhi_ref, ga_ref, wq_ref, wo_ref, ya_ref,
               k_buf, v_buf, o_buf, nb=nb, lq=lq, new_memory=None)


def _ffn_and_attn(x_ffn, x_attn, mem_k, mem_v, layer, lw, g_final, *, tm, final_norm):
    t = x_ffn.shape[0]
    n_seq, lq, _ = x_attn.shape
    assert t % tm == 0 and n_seq % (t // tm) == 0 and lq % SUBLANES == 0
    steps = t // tm
    nb = n_seq // steps
    row_tile = pl.BlockSpec((tm, D_MODEL), lambda i: (i, 0))
    seq_tile = pl.BlockSpec((nb, lq, D_MODEL), lambda i: (i, 0, 0))

    def mem_half(part):
        return pl.BlockSpec((None, nb, N_MEM, N_XHEADS, LANES), lambda i: (layer, i, 0, 0, part))

    return pl.pallas_call(
        functools.partial(_ffn_and_attn_kernel, final_norm=final_norm, nb=nb, lq=lq),
        grid=(steps,),
        in_specs=[row_tile, _layer_resident(lw['g_ffn'], layer), _layer_resident(lw['w_ff1'], layer),
                  _layer_resident(lw['w_ff2'], layer), _resident(g_final.shape),
                  seq_tile, mem_half(0), mem_half(1), mem_half(0), mem_half(1),
                  _layer_resident(lw['g_x'], layer), _layer_resident(lw['w_xq'], layer),
                  _layer_resident(lw['w_xo'], layer)],
        out_specs=[row_tile, seq_tile],
        out_shape=[jax.ShapeDtypeStruct(x_ffn.shape, F32), jax.ShapeDtypeStruct(x_attn.shape, F32)],
        scratch_shapes=[pltpu.VMEM((nb, N_MEM, D_MODEL), BF16),
                        pltpu.VMEM((nb, N_MEM, D_MODEL), BF16),
                        pltpu.VMEM((nb * lq, D_MODEL), F32)],
        compiler_params=pltpu.CompilerParams(
            dimension_semantics=("arbitrary",), vmem_limit_bytes=VMEM_LIMIT_BYTES),
        name="ffn_and_attn",
    )(x_ffn, lw['g_ffn'], lw['w_ff1'], lw['w_ff2'], g_final,
      x_attn, mem_k, mem_k, mem_v, mem_v, lw['g_x'], lw['w_xq'], lw['w_xo'])


def _mem_kv_kernel(m_ref, g_ref, *refs, nb, n_cast):
    wk_refs, wv_refs = refs[:N_XHEADS], refs[N_XHEADS:2 * N_XHEADS]
    cast_in, refs = refs[2 * N_XHEADS:2 * N_XHEADS + n_cast], refs[2 * N_XHEADS + n_cast:]
    k_ref, v_ref, *cast_out = refs
    _cast_blocks(cast_in, cast_out)
    m = _rmsnorm(m_ref[...], g_ref[...]).astype(BF16)
    for w_refs, o_ref in ((wk_refs, k_ref), (wv_refs, v_ref)):
        cols = jnp.concatenate([r[...].astype(BF16) for r in w_refs], axis=-1)
        y = _dot(m, cols)
        flat = o_ref.reshape(nb * N_MEM * N_XHEADS, LANES)
        for b in range(nb):
            for hd in range(N_XHEADS):
                rows = pl.ds(b * N_MEM * N_XHEADS + hd, N_MEM, stride=N_XHEADS)
                flat[rows, :] = y[b * N_MEM:(b + 1) * N_MEM, hd * LANES:(hd + 1) * LANES]


def _mem_kv(mem, g, w_k, w_v, *, nb, also_cast=()):
    n_seq = mem.shape[0]
    assert n_seq % nb == 0 and XHEAD_DIM == 2 * LANES
    rows = nb * N_MEM
    n_i = n_seq // nb
    shape = (DEPTH, n_seq, N_MEM, N_XHEADS, XHEAD_DIM)
    cast_in_specs, cast_out_specs, cast_shapes = _side_casts(
        also_cast, DEPTH * 2 * n_i, lambda l, part, i: (l * 2 + part) * n_i + i)

    def head_half_cols(hd):
        return pl.BlockSpec((None, D_MODEL, LANES), lambda l, part, i: (l, 0, 2 * hd + part))

    w_specs = [head_half_cols(hd) for hd in range(N_XHEADS)]
    out_half = pl.BlockSpec((None, nb, N_MEM, N_XHEADS, LANES), lambda l, part, i: (l, i, 0, 0, part))
    outs = pl.pallas_call(
        functools.partial(_mem_kv_kernel, nb=nb, n_cast=len(also_cast)),
        grid=(DEPTH, 2, n_i),
        in_specs=[pl.BlockSpec((rows, D_MODEL), lambda l, part, i: (i, 0)),
                  pl.BlockSpec((None, 1, D_MODEL), lambda l, part, i: (l, 0, 0))]
                 + w_specs + w_specs + cast_in_specs,
        out_specs=[out_half] * 2 + cast_out_specs,
        out_shape=[jax.ShapeDtypeStruct(shape, F32)] * 2 + cast_shapes,
        compiler_params=pltpu.CompilerParams(
            dimension_semantics=("arbitrary", "arbitrary", "arbitrary"),
            vmem_limit_bytes=VMEM_LIMIT_BYTES),
        name="mem_kv",
    )(mem.reshape(n_seq * N_MEM, D_MODEL), g, *([w_k] * N_XHEADS), *([w_v] * N_XHEADS),
      *(a for a, _, _ in also_cast))
    return outs[0], outs[1], tuple(outs[2:])


def kernel(x_prompt, x_sample, state_pool, state_sconv, state_cconv, cache_mem_k, cache_mem_v, mem_prompt, norm_mix_g, w_in, pool_w, pool_scale, sconv_w, sgu_ln_g, sgu_ln_b, sgu_w, sgu_b, cconv_w, cconv_b, cconv_ln_g, cconv_ln_b, w_branch, b_gate, w_mix_out, norm_x_g, norm_mem_g, w_xq, w_xk, w_xv, w_xo, norm_ffn_g, w_ff1, w_ff2, norm_f_g):
    batch, seq, _ = x_prompt.shape
    dec_batch, dec_seq, _ = x_sample.shape

    def rows(v):
        return v.reshape(DEPTH, 1, -1)

    reps = CHUNK // dec_seq
    small = {
        'g_mix': rows(norm_mix_g), 'pool_w': pool_w.astype(BF16), 'pool_scale': rows(pool_scale),
        'sconv_w': sconv_w, 'sgu_ln_g': rows(sgu_ln_g), 'sgu_ln_b': rows(sgu_ln_b),
        'cconv_w': cconv_w, 'cconv_b': rows(cconv_b),
        'cconv_ln_g': rows(cconv_ln_g), 'cconv_ln_b': rows(cconv_ln_b),
        'b_gate': b_gate, 'g_x': rows(norm_x_g), 'g_ffn': rows(norm_ffn_g),
    }
    sgu_p = {'sgu_w': sgu_w, 'sgu_bcol': jnp.swapaxes(sgu_b, 1, 2)}
    sgu_s = {'sgu_w': jnp.tile(sgu_w[:, :, :dec_seq, :dec_seq], (1, 1, reps, reps)),
             'sgu_bcol': jnp.swapaxes(jnp.tile(sgu_b[:, :, :dec_seq], (1, 1, reps)), 1, 2)}
    big = {'w_in': w_in, 'w_branch': w_branch, 'w_mix_out': w_mix_out, 'w_xq': w_xq, 'w_xo': w_xo}
    as_rows = lambda w: w.reshape(-1, w.shape[-1])
    per_layer = lambda w: as_rows(w).shape[0] // DEPTH
    first = [(as_rows(w), 0, per_layer(w)) for w in big.values()]
    later = [(as_rows(w_ff1), 0, DEPTH * D_MODEL), (as_rows(w_ff2), 0, DEPTH * D_FF)]
    later += [(as_rows(w), per_layer(w), per_layer(w)) for w in big.values()]
    packed_shape = lambda w: w.shape[:-2] + (w.shape[-2] // 2, w.shape[-1])
    one_layer = lambda cast: {k: c.reshape((1,) + packed_shape(w)[1:])
                              for (k, w), c in zip(big.items(), cast)}
    g_final = norm_f_g.reshape(1, -1)

    mem_k_prompt, mem_v_prompt, cast = _mem_kv(mem_prompt, rows(norm_mem_g), w_xk, w_xv, nb=4,
                                               also_cast=first)
    weights = one_layer(cast)
    zeros = lambda n_seq, n: jnp.zeros((DEPTH, n_seq, n, W_BR), F32)
    hist_p = (zeros(batch, POOL_PREV), zeros(batch, SCONV_K - 1), zeros(batch, CCONV_K - 1))
    hist_s = (state_pool, state_sconv, state_cconv)
    new_p = tuple(jnp.zeros_like(s) for s in hist_p)
    new_s = tuple(jnp.zeros_like(s) for s in hist_s) + (zeros(dec_batch, dec_seq),)
    x_p, x_s = x_prompt, x_sample
    for l in range(DEPTH):
        last = l == DEPTH - 1
        lw_p, lw_s = dict(small, **sgu_p, **weights), dict(small, **sgu_s, **weights)
        x_p, new_p = _mixer(x_p, *hist_p, l, new_p, lw_p, nb=1, lc=512, pos0=0, emit_v=False)
        x_p, cast = _attn(x_p, mem_k_prompt, mem_v_prompt, l, lw_p['g_x'], lw_p['w_xq'], lw_p['w_xo'],
                          nb=1, lq=1024, also_cast=() if l else later)
        if cast:
            ffn_w = {'w_ff1': cast[0].reshape(packed_shape(w_ff1)),
                     'w_ff2': cast[1].reshape(packed_shape(w_ff2))}
            next_weights = one_layer(cast[2:])
        x_s, new_s = _mixer(x_s, *hist_s, l, new_s, lw_s, nb=256 // dec_seq, lc=dec_seq,
                            pos0=PAST_LEN, emit_v=True)
        lw_ffn = dict(lw_p, **ffn_w)
        x_p, x_s = _ffn_and_attn(x_p.reshape(batch * seq, D_MODEL), x_s, cache_mem_k, cache_mem_v,
                                 l, lw_ffn, g_final, tm=512, final_norm=last)
        x_p = x_p.reshape(batch, seq, D_MODEL)
        x_s = _ffn(x_s.reshape(dec_batch * dec_seq, D_MODEL), l, lw_ffn['g_ffn'], lw_ffn['w_ff1'],
                   lw_ffn['w_ff2'], g_final, tm=512, final_norm=last).reshape(x_sample.shape)
        weights = next_weights

    return (x_p, x_s) + new_p + (mem_k_prompt, mem_v_prompt) + new_s
```
